```python
import math
import jax, jax.numpy as jnp
from jax import lax
import numpy as np

D_MODEL = 1024
BATCH = 4
SEQ = 8192
DEPTH = 2

GRID_W = 64
CTX_LEN = 256
N_MIXERS = 2
MIXER_MLA = 0
MIXER_CONV = 1
N_HEADS = 8
QK_NOPE = 128
QK_ROPE = 64
V_HEAD = 128
Q_LORA = 256
KV_LORA = 128
CONV_W = 3
D_FF = ((8 * D_MODEL + 3 * 256 - 1) // (3 * 256)) * 256
ROPE_THETA = 10000.0
NORM_EPS = 1e-6
Q_BLOCK = 128
ATTN_SCALE = 1.0 / math.sqrt(QK_NOPE + QK_ROPE)
N_MLA_LAYERS = (DEPTH + N_MIXERS - 1) // N_MIXERS
N_CONV_LAYERS = DEPTH // N_MIXERS

kernel_name = "hybrid_mla_shortconv_dit_block"


def rmsnorm(x, w):
    xf = x.astype(jnp.float32)
    y = xf * lax.rsqrt(jnp.mean(xf * xf, axis=-1, keepdims=True) + NORM_EPS)
    return (y * w.astype(jnp.float32)).astype(x.dtype)


def modulate(h, shift, scale):
    return h * (1 + scale) + shift


def axial_angles(n_tok):
    rows = n_tok // GRID_W
    row, col = jnp.meshgrid(jnp.arange(rows), jnp.arange(GRID_W), indexing="ij")
    row = row.reshape(-1).astype(jnp.float32)
    col = col.reshape(-1).astype(jnp.float32)
    n_freq = QK_ROPE // 4
    freqs = ROPE_THETA ** (-jnp.arange(n_freq, dtype=jnp.float32) / n_freq)
    return row[:, None] * freqs, col[:, None] * freqs


def rotate_half(x, ang):
    x1, x2 = jnp.split(x, 2, axis=-1)
    cos, sin = jnp.cos(ang), jnp.sin(ang)
    return jnp.concatenate([x1 * cos - x2 * sin, x2 * cos + x1 * sin], axis=-1)


def axial_rope(x, ang_row, ang_col):
    extra = x.ndim - 3
    shp = (ang_row.shape[0],) + (1,) * extra + (ang_row.shape[1],)
    xr, xc = jnp.split(x.astype(jnp.float32), 2, axis=-1)
    out = jnp.concatenate([rotate_half(xr, ang_row.reshape(shp)),
                           rotate_half(xc, ang_col.reshape(shp))], axis=-1)
    return out.astype(x.dtype)


def mla_qkv(h, w_dq, q_norm, w_uq, w_dkv, kv_norm, w_ukv):
    b, n, _ = h.shape
    q = (rmsnorm(h @ w_dq, q_norm) @ w_uq).reshape(b, n, N_HEADS, QK_NOPE + QK_ROPE)
    kv_a = h @ w_dkv
    kv = (rmsnorm(kv_a[..., :KV_LORA], kv_norm) @ w_ukv).reshape(b, n, N_HEADS, QK_NOPE + V_HEAD)
    return (q[..., :QK_NOPE], q[..., QK_NOPE:],
            kv[..., :QK_NOPE], kv_a[..., KV_LORA:], kv[..., QK_NOPE:])


def attend(q_nope, q_rope, k_nope, k_rope, v):
    s = (jnp.einsum("bqhd,bkhd->bhqk", q_nope, k_nope)
         + jnp.einsum("bqhr,bkr->bhqk", q_rope, k_rope))
    p = jax.nn.softmax(s.astype(jnp.float32) * ATTN_SCALE, axis=-1).astype(v.dtype)
    return jnp.einsum("bhqk,bkhd->bqhd", p, v)


def mla_mixer(h_lat, h_ctx, w_dq, q_norm, w_uq, w_dkv, kv_norm, w_ukv, w_o,
              ang_row, ang_col, with_ctx_out):
    b, n, _ = h_lat.shape
    qn_l, qr_l, kn_l, kr_l, v_l = mla_qkv(h_lat, w_dq, q_norm, w_uq, w_dkv, kv_norm, w_ukv)
    qr_l = axial_rope(qr_l, ang_row, ang_col)
    kr_l = axial_rope(kr_l, ang_row, ang_col)
    qn_c, qr_c, kn_c, kr_c, v_c = mla_qkv(h_ctx, w_dq, q_norm, w_uq, w_dkv, kv_norm, w_ukv)
    k_nope = jnp.concatenate([kn_c, kn_l], axis=1)
    k_rope = jnp.concatenate([kr_c, kr_l], axis=1)
    v = jnp.concatenate([v_c, v_l], axis=1)
    nb = n // Q_BLOCK

    def to_blocks(t):
        return jnp.moveaxis(t.reshape(b, nb, Q_BLOCK, *t.shape[2:]), 1, 0)

    o = lax.map(lambda q: attend(q[0], q[1], k_nope, k_rope, v), (to_blocks(qn_l), to_blocks(qr_l)))
    o = jnp.moveaxis(o, 0, 1).reshape(b, n, N_HEADS * V_HEAD)
    y_lat = o @ w_o
    y_ctx = None
    if with_ctx_out:
        o_c = attend(qn_c, qr_c, kn_c, kr_c, v_c)
        y_ctx = o_c.reshape(b, h_ctx.shape[1], N_HEADS * V_HEAD) @ w_o
    return y_lat, y_ctx


def short_conv(h, w_in, w_conv, w_out):
    b_g, c_g, xv = jnp.split(h @ w_in, 3, axis=-1)
    u = lax.conv_general_dilated(c_g * xv, w_conv[:, None, :], window_strides=(1,),
                                 padding=[(CONV_W // 2, CONV_W // 2)],
                                 dimension_numbers=("NWC", "WIO", "NWC"),
                                 feature_group_count=D_MODEL)
    return (b_g * u) @ w_out


def swiglu(h, w_gate, w_up, w_down):
    return (jax.nn.silu(h @ w_gate) * (h @ w_up)) @ w_down


def setup_inputs(seed: int = 0) -> dict:
    key = jax.random.key(seed)
    ks = jax.random.split(key, 24)
    f32 = jnp.float32

    def nrm(k, shape, scale):
        return jax.random.normal(k, shape, f32) * scale

    def gain(k, shape):
        return 1.0 + 0.02 * jax.random.normal(k, shape, f32)

    d = D_MODEL
    return {
        "x": nrm(ks[0], (BATCH, SEQ, d), 1.0),
        "c": nrm(ks[1], (BATCH, d), 1.0),
        "ctx": nrm(ks[2], (BATCH, CTX_LEN, d), 1.0),
        "c_ctx": nrm(ks[3], (d,), 1.0),
        "ada_w": nrm(ks[4], (DEPTH, d, 6 * d), 0.5 * d ** -0.5),
        "ada_b": nrm(ks[5], (DEPTH, 6 * d), 0.02),
        "norm_mix": gain(ks[6], (DEPTH, d)),
        "norm_ffn": gain(ks[7], (DEPTH, d)),
        "mla_w_dq": nrm(ks[8], (N_MLA_LAYERS, d, Q_LORA), d ** -0.5),
        "mla_q_norm": gain(ks[9], (N_MLA_LAYERS, Q_LORA)),
        "mla_w_uq": nrm(ks[10], (N_MLA_LAYERS, Q_LORA, N_HEADS * (QK_NOPE + QK_ROPE)), Q_LORA ** -0.5),
        "mla_w_dkv": nrm(ks[11], (N_MLA_LAYERS, d, KV_LORA + QK_ROPE), d ** -0.5),
        "mla_kv_norm": gain(ks[12], (N_MLA_LAYERS, KV_LORA)),
        "mla_w_ukv": nrm(ks[13], (N_MLA_LAYERS, KV_LORA, N_HEADS * (QK_NOPE + V_HEAD)), KV_LORA ** -0.5),
        "mla_w_o": nrm(ks[14], (N_MLA_LAYERS, N_HEADS * V_HEAD, d), (N_HEADS * V_HEAD) ** -0.5),
        "conv_w_in": nrm(ks[15], (N_CONV_LAYERS, d, 3 * d), d ** -0.5),
        "conv_w": nrm(ks[16], (N_CONV_LAYERS, CONV_W, d), CONV_W ** -0.5),
        "conv_w_out": nrm(ks[17], (N_CONV_LAYERS, d, d), d ** -0.5),
        "ffn_w_gate": nrm(ks[18], (DEPTH, d, D_FF), d ** -0.5),
        "ffn_w_up": nrm(ks[19], (DEPTH, d, D_FF), d ** -0.5),
        "ffn_w_down": nrm(ks[20], (DEPTH, D_FF, d), D_FF ** -0.5),
        "final_norm": gain(ks[21], (d,)),
    }


def reference(x, c, ctx, c_ctx, ada_w, ada_b, norm_mix, norm_ffn,
              mla_w_dq, mla_q_norm, mla_w_uq, mla_w_dkv, mla_kv_norm, mla_w_ukv, mla_w_o,
              conv_w_in, conv_w, conv_w_out, ffn_w_gate, ffn_w_up, ffn_w_down, final_norm):
    d = D_MODEL
    ang_row, ang_col = axial_angles(x.shape[1])
    silu_c = jax.nn.silu(c)
    silu_cc = jax.nn.silu(c_ctx)
    for i in range(DEPTH):
        mixer = i % N_MIXERS
        j = i // N_MIXERS
        ctx_live = any(k % N_MIXERS == MIXER_MLA for k in range(i + 1, DEPTH))
        sh1, sc1, g1, sh2, sc2, g2 = jnp.split((silu_c @ ada_w[i] + ada_b[i])[:, None, :], 6, axis=-1)
        n_ctx_mod = 6 if ctx_live else (2 if mixer == MIXER_MLA else 0)
        mc = None
        if n_ctx_mod:
            mc = jnp.split(silu_cc @ ada_w[i][:, :n_ctx_mod * d] + ada_b[i][:n_ctx_mod * d], n_ctx_mod)
        h = modulate(rmsnorm(x, norm_mix[i]), sh1, sc1)
        if mixer == MIXER_MLA:
            h_c = modulate(rmsnorm(ctx, norm_mix[i]), mc[0], mc[1])
            y, y_c = mla_mixer(h, h_c, mla_w_dq[j], mla_q_norm[j], mla_w_uq[j], mla_w_dkv[j],
                               mla_kv_norm[j], mla_w_ukv[j], mla_w_o[j], ang_row, ang_col, ctx_live)
        else:
            y = short_conv(h, conv_w_in[j], conv_w[j], conv_w_out[j])
            y_c = None
            if ctx_live:
                h_c = modulate(rmsnorm(ctx, norm_mix[i]), mc[0], mc[1])
                y_c = short_conv(h_c, conv_w_in[j], conv_w[j], conv_w_out[j])
        x = x + g1 * y
        x = x + g2 * swiglu(modulate(rmsnorm(x, norm_ffn[i]), sh2, sc2),
                            ffn_w_gate[i], ffn_w_up[i], ffn_w_down[i])
        if ctx_live:
            ctx = ctx + mc[2] * y_c
            ctx = ctx + mc[5] * swiglu(modulate(rmsnorm(ctx, norm_ffn[i]), mc[3], mc[4]),
                                       ffn_w_gate[i], ffn_w_up[i], ffn_w_down[i])
    return rmsnorm(x, final_norm)
```

```python
import functools
import math

import numpy as np
import jax
import jax.numpy as jnp
from jax import lax
from jax.experimental import pallas as pl
from jax.experimental.pallas import tpu as pltpu

D_MODEL = 1024
GRID_W = 64
N_HEADS = 8
QK_NOPE = 128
QK_ROPE = 64
V_HEAD = 128
Q_LORA = 256
KV_LORA = 128
QK_DIM = QK_NOPE + QK_ROPE
ROPE_THETA = 10000.0
NORM_EPS = 1e-6
ATTN_SCALE = 1.0 / math.sqrt(QK_NOPE + QK_ROPE)
LOG2_E = 1.4426950408889634

F32 = jnp.float32
BF16 = jnp.bfloat16

VMEM_LIMIT_BYTES = 56 * 1024 * 1024
HALO = 8


def _params(*sem):
    return pltpu.CompilerParams(dimension_semantics=sem, vmem_limit_bytes=VMEM_LIMIT_BYTES)


def _resident(shape):
    zeros = (0,) * len(shape)
    return pl.BlockSpec(shape, lambda *_: zeros, pipeline_mode=pl.Buffered(1))


def _rms(x, w):
    return x * lax.rsqrt(jnp.mean(x * x, axis=-1, keepdims=True) + NORM_EPS) * w


def _dot(a, b):
    return jnp.dot(a, b, preferred_element_type=F32)


def _mod_kernel(c_ref, w_ref, b_ref, o_ref):
    c = c_ref[...]
    o_ref[0] = _dot(c * jax.nn.sigmoid(c), w_ref[0]) + b_ref[0]


def _modulation(c_rows, ada_w, ada_b):
    depth, d, d6 = ada_w.shape
    rows = c_rows.shape[0]
    tn = 1536
    return pl.pallas_call(
        _mod_kernel,
        grid=(depth, d6 // tn),
        in_specs=[
            pl.BlockSpec((rows, d), lambda i, n: (0, 0)),
            pl.BlockSpec((1, d, tn), lambda i, n: (i, 0, n)),
            pl.BlockSpec((1, 1, tn), lambda i, n: (i, 0, n)),
        ],
        out_specs=pl.BlockSpec((1, rows, tn), lambda i, n: (i, 0, n)),
        out_shape=jax.ShapeDtypeStruct((depth, rows, d6), F32),
        compiler_params=_params("arbitrary", "arbitrary"),
        name="adaln_modulation",
    )(c_rows, ada_w, ada_b.reshape(depth, 1, d6))


def _qkv_kernel(x_ref, ctx_ref, mod_ref, nw_ref, qn_ref, kvn_ref, tc_ref, ts_ref,
                wdn_ref, wuq_ref, wukv_ref, q_ref, k_ref, v_ref):
    j = pl.program_id(1)
    xin = jnp.where(j == 0, ctx_ref[0], x_ref[0])
    shift = mod_ref[0, 0, 0:1, :]
    scale = mod_ref[0, 0, 1:2, :]
    h = (_rms(xin, nw_ref[...]) * (1.0 + scale) + shift).astype(BF16)
    down = _dot(h, wdn_ref[...])
    tc = tc_ref[...]
    ts = ts_ref[...]

    ckv = _rms(down[:, Q_LORA:Q_LORA + KV_LORA], kvn_ref[...]).astype(BF16)
    kr0 = Q_LORA + KV_LORA
    k_rope = (down[:, kr0:kr0 + QK_ROPE] * tc[:, :QK_ROPE]
              + down[:, kr0 + QK_ROPE:kr0 + 2 * QK_ROPE] * ts[:, :QK_ROPE]).astype(BF16)
    kv = _dot(ckv, wukv_ref[...])
    for hd in range(N_HEADS):
        k_ref[0, hd, :, 0:QK_NOPE] = kv[:, hd * QK_NOPE:(hd + 1) * QK_NOPE].astype(BF16)
        k_ref[0, hd, :, QK_NOPE:QK_DIM] = k_rope
        v0 = N_HEADS * QK_NOPE + hd * V_HEAD
        v_ref[0, hd] = kv[:, v0:v0 + V_HEAD].astype(BF16)

    @pl.when(j > 0)
    def _():
        cq = _rms(down[:, 0:Q_LORA], qn_ref[...]).astype(BF16)
        qf = _dot(cq, wuq_ref[...]) * (ATTN_SCALE * LOG2_E)
        r0 = N_HEADS * QK_NOPE
        r1 = r0 + N_HEADS * QK_ROPE
        for pair in range(N_HEADS // 2):
            lo = 2 * QK_ROPE * pair
            rope2 = (qf[:, r0 + lo:r0 + lo + 2 * QK_ROPE] * tc
                     + qf[:, r1 + lo:r1 + lo + 2 * QK_ROPE] * ts).astype(BF16)
            for sub in range(2):
                hd = 2 * pair + sub
                q_ref[0, hd, :, 0:QK_NOPE] = qf[:, hd * QK_NOPE:(hd + 1) * QK_NOPE].astype(BF16)
                q_ref[0, hd, :, QK_NOPE:QK_DIM] = rope2[:, sub * QK_ROPE:(sub + 1) * QK_ROPE]


def _mla_qkv(x, ctx, mod, norm_w, q_norm, kv_norm, tc, ts, w_down, w_uq, w_ukv):
    b, n, d = x.shape
    n_ctx = ctx.shape[1]
    tm = n_ctx
    assert n % tm == 0
    n_all = n_ctx + n
    grid = (b, n_all // tm)
    lat = lambda bi, j: (bi, jnp.maximum(j - 1, 0), 0)
    return pl.pallas_call(
        _qkv_kernel,
        grid=grid,
        in_specs=[
            pl.BlockSpec((1, tm, d), lat),
            pl.BlockSpec((1, n_ctx, d), lambda bi, j: (bi, 0, 0)),
            pl.BlockSpec((1, 1, 2, d), lambda bi, j: (bi, jnp.minimum(j, 1), 0, 0)),
            _resident((1, d)),
            _resident((1, Q_LORA)),
            _resident((1, KV_LORA)),
            pl.BlockSpec((tm, 2 * QK_ROPE), lambda bi, j: (j, 0)),
            pl.BlockSpec((tm, 2 * QK_ROPE), lambda bi, j: (j, 0)),
            _resident(w_down.shape),
            _resident(w_uq.shape),
            _resident(w_ukv.shape),
        ],
        out_specs=[
            pl.BlockSpec((1, N_HEADS, tm, QK_DIM), lambda bi, j: (bi, 0, jnp.maximum(j - 1, 0), 0)),
            pl.BlockSpec((1, N_HEADS, tm, QK_DIM), lambda bi, j: (bi, 0, j, 0)),
            pl.BlockSpec((1, N_HEADS, tm, V_HEAD), lambda bi, j: (bi, 0, j, 0)),
        ],
        out_shape=[
            jax.ShapeDtypeStruct((b, N_HEADS, n, QK_DIM), BF16),
            jax.ShapeDtypeStruct((b, N_HEADS, n_all, QK_DIM), BF16),
            jax.ShapeDtypeStruct((b, N_HEADS, n_all, V_HEAD), BF16),
        ],
        compiler_params=_params("arbitrary", "arbitrary"),
        name="mla_qkv",
    )(x, ctx, mod, norm_w, q_norm, kv_norm, tc, ts, w_down, w_uq, w_ukv)


def _attn_kernel(q_ref, k_ref, v_ref, o_ref, *, tk):
    q = q_ref[0, 0]
    tq = q.shape[0]
    n_keys = k_ref.shape[2]
    m = jnp.full((tq, 1), -jnp.inf, F32)
    l = jnp.zeros((tq, 1), F32)
    acc = jnp.zeros((tq, V_HEAD), F32)
    for c in range(n_keys // tk):
        kc = k_ref[0, 0, c * tk:(c + 1) * tk, :]
        vc = v_ref[0, 0, c * tk:(c + 1) * tk, :]
        s = lax.dot_general(q, kc, (((1,), (1,)), ((), ())), preferred_element_type=F32)
        m_new = jnp.maximum(m, jnp.max(s, axis=-1, keepdims=True))
        alpha = jnp.exp2(m - m_new)
        p = jnp.exp2(s - m_new)
        l = alpha * l + jnp.sum(p, axis=-1, keepdims=True)
        acc = alpha * acc + _dot(p.astype(BF16), vc)
        m = m_new
    o_ref[0] = (acc / l).astype(BF16)


def _attention(q, k, v, tq, tk):
    b, nh, n, _ = q.shape
    n_all = k.shape[2]
    assert n % tq == 0 and n_all % tk == 0
    return pl.pallas_call(
        functools.partial(_attn_kernel, tk=tk),
        grid=(b, nh, n // tq),
        in_specs=[
            pl.BlockSpec((1, 1, tq, QK_DIM), lambda bi, hi, i: (bi, hi, i, 0)),
            pl.BlockSpec((1, 1, n_all, QK_DIM), lambda bi, hi, i: (bi, hi, 0, 0)),
            pl.BlockSpec((1, 1, n_all, V_HEAD), lambda bi, hi, i: (bi, hi, 0, 0)),
        ],
        out_specs=pl.BlockSpec((1, tq, V_HEAD), lambda bi, hi, i: (bi, i, hi)),
        out_shape=jax.ShapeDtypeStruct((b, n, nh * V_HEAD), BF16),
        compiler_params=_params("arbitrary", "arbitrary", "arbitrary"),
        name="mla_attention",
    )(q, k, v)


def _ffn_kernel(*refs, with_proj, with_final):
    refs = list(refs)
    x_ref = refs.pop(0)
    o_ref = refs.pop(0) if with_proj else None
    mod_ref = refs.pop(0)
    nw_ref = refs.pop(0)
    wo_ref = refs.pop(0) if with_proj else None
    wg_ref, wu_ref, wd_ref = refs.pop(0), refs.pop(0), refs.pop(0)
    fw_ref = refs.pop(0) if with_final else None
    out_ref = refs.pop(0)

    x = x_ref[0]
    if with_proj:
        x = x + mod_ref[0, 0:1, :] * _dot(o_ref[0], wo_ref[...])
    shift, scale, gate = mod_ref[0, 1:2, :], mod_ref[0, 2:3, :], mod_ref[0, 3:4, :]
    h = (_rms(x, nw_ref[...]) * (1.0 + scale) + shift).astype(BF16)
    g = _dot(h, wg_ref[...])
    u = _dot(h, wu_ref[...])
    act = (g * jax.nn.sigmoid(g) * u).astype(BF16)
    x = x + gate * _dot(act, wd_ref[...])
    if with_final:
        x = _rms(x, fw_ref[...])
    out_ref[0] = x


def _ffn(x, o, mod, norm_w, w_o, w_gate, w_up, w_down, final_w, tm):
    b, n, d = x.shape
    with_proj = o is not None
    with_final = final_w is not None
    tile = pl.BlockSpec((1, tm, d), lambda bi, i: (bi, i, 0))
    args, specs = [x], [tile]
    if with_proj:
        args.append(o)
        specs.append(tile)
    args += [mod, norm_w]
    specs += [pl.BlockSpec((1, 4, d), lambda bi, i: (bi, 0, 0)), _resident((1, d))]
    if with_proj:
        args.append(w_o)
        specs.append(_resident(w_o.shape))
    args += [w_gate, w_up, w_down]
    specs += [_resident(w_gate.shape), _resident(w_up.shape), _resident(w_down.shape)]
    if with_final:
        args.append(final_w)
        specs.append(_resident((1, d)))
    return pl.pallas_call(
        functools.partial(_ffn_kernel, with_proj=with_proj, with_final=with_final),
        grid=(b, n // tm),
        in_specs=specs,
        out_specs=tile,
        out_shape=jax.ShapeDtypeStruct((b, n, d), F32),
        compiler_params=_params("arbitrary", "arbitrary"),
        name="attn_proj_ffn" if with_proj else "ffn_final",
    )(*args)


def _conv_kernel(x_ref, xp_ref, xn_ref, mod_ref, nw_ref, win_ref, cw_ref, wout_ref, out_ref, z_ref):
    i = pl.program_id(1)
    last = pl.num_programs(1) - 1
    d = x_ref.shape[2]
    tm = x_ref.shape[1]
    x = x_ref[0]
    xe = jnp.concatenate([xp_ref[0], x, xn_ref[0]], axis=0)
    shift, scale, gate = mod_ref[0, 0:1, :], mod_ref[0, 1:2, :], mod_ref[0, 2:3, :]
    h = (_rms(xe, nw_ref[...]) * (1.0 + scale) + shift).astype(BF16)
    p = _dot(h, win_ref[...])
    z = p[:, d:2 * d] * p[:, 2 * d:3 * d]
    row = lax.broadcasted_iota(jnp.int32, (tm + 2 * HALO, 1), 0)
    outside = ((row < HALO) & (i == 0)) | ((row >= tm + HALO) & (i == last))
    z_ref[...] = jnp.where(outside, 0.0, z)
    u = (cw_ref[0:1, :] * z_ref[HALO - 1:HALO - 1 + tm, :]
         + cw_ref[1:2, :] * z_ref[HALO:HALO + tm, :]
         + cw_ref[2:3, :] * z_ref[HALO + 1:HALO + 1 + tm, :])
    y = _dot((p[HALO:HALO + tm, 0:d] * u).astype(BF16), wout_ref[...])
    out_ref[0] = x + gate * y


def _short_conv(x, mod, norm_w, w_in, conv_w, w_out, tm):
    b, n, d = x.shape
    nblk = n // HALO
    per = tm // HALO
    return pl.pallas_call(
        _conv_kernel,
        grid=(b, n // tm),
        in_specs=[
            pl.BlockSpec((1, tm, d), lambda bi, i: (bi, i, 0)),
            pl.BlockSpec((1, HALO, d), lambda bi, i: (bi, jnp.maximum(i * per - 1, 0), 0)),
            pl.BlockSpec((1, HALO, d), lambda bi, i: (bi, jnp.minimum((i + 1) * per, nblk - 1), 0)),
            pl.BlockSpec((1, 3, d), lambda bi, i: (bi, 0, 0)),
            _resident((1, d)),
            _resident(w_in.shape),
            _resident(conv_w.shape),
            _resident(w_out.shape),
        ],
        out_specs=pl.BlockSpec((1, tm, d), lambda bi, i: (bi, i, 0)),
        out_shape=jax.ShapeDtypeStruct((b, n, d), F32),
        scratch_shapes=[pltpu.VMEM((tm + 2 * HALO, d), F32)],
        compiler_params=_params("arbitrary", "arbitrary"),
        name="short_conv",
    )(x, x, x, mod, norm_w, w_in, conv_w, w_out)


def _rope_tables(n, n_ctx):
    rows = n // GRID_W
    row, col = jnp.meshgrid(jnp.arange(rows), jnp.arange(GRID_W), indexing="ij")
    row = row.reshape(-1).astype(F32)
    col = col.reshape(-1).astype(F32)
    n_freq = QK_ROPE // 4
    freqs = ROPE_THETA ** (-jnp.arange(n_freq, dtype=F32) / n_freq)
    ar, ac = row[:, None] * freqs, col[:, None] * freqs
    cos = jnp.concatenate([jnp.cos(ar), jnp.cos(ar), jnp.cos(ac), jnp.cos(ac)], axis=-1)
    sin = jnp.concatenate([jnp.sin(ar), jnp.sin(ar), jnp.sin(ac), jnp.sin(ac)], axis=-1)
    cos = jnp.concatenate([jnp.ones((n_ctx, QK_ROPE), F32), cos], axis=0)
    sin = jnp.concatenate([jnp.zeros((n_ctx, QK_ROPE), F32), sin], axis=0)
    return jnp.tile(cos, (1, 2)), jnp.tile(sin, (1, 2))


def _rotate_cols(w):
    q = QK_ROPE // 4
    perm = np.concatenate([np.arange(q, 2 * q), np.arange(0, q), np.arange(3 * q, 4 * q), np.arange(2 * q, 3 * q)])
    sign = np.concatenate([-np.ones(q), np.ones(q), -np.ones(q), np.ones(q)]).astype(np.float32)
    return w[..., perm] * sign


def kernel(x, c, ctx, c_ctx, ada_w, ada_b, norm_mix, norm_ffn, mla_w_dq, mla_q_norm, mla_w_uq, mla_w_dkv,
           mla_kv_norm, mla_w_ukv, mla_w_o, conv_w_in, conv_w, conv_w_out, ffn_w_gate, ffn_w_up, ffn_w_down,
           final_norm):
    b, n, d = x.shape
    n_ctx = ctx.shape[1]

    pad = (-(b + 1)) % 8
    c_rows = jnp.concatenate([c, c_ctx[None, :], jnp.zeros((pad, d), F32)], axis=0)
    mods = _modulation(c_rows, ada_w, ada_b)
    m0 = mods[0, :b].reshape(b, 6, d)
    m1 = mods[1, :b].reshape(b, 6, d)
    mc = jnp.broadcast_to(mods[0, b].reshape(1, 6, d)[:, 0:2], (b, 2, d))
    mod_qkv = jnp.stack([mc, m0[:, 0:2]], axis=1)

    w_uq = mla_w_uq[0].reshape(Q_LORA, N_HEADS, QK_DIM)
    uq_rope = w_uq[:, :, QK_NOPE:]
    w_uq_ext = jnp.concatenate([w_uq[:, :, :QK_NOPE].reshape(Q_LORA, -1), uq_rope.reshape(Q_LORA, -1),
                                _rotate_cols(uq_rope).reshape(Q_LORA, -1)], axis=1).astype(BF16)
    w_dkv = mla_w_dkv[0]
    w_down = jnp.concatenate([mla_w_dq[0], w_dkv, _rotate_cols(w_dkv[:, KV_LORA:])], axis=1).astype(BF16)
    w_ukv = mla_w_ukv[0].reshape(KV_LORA, N_HEADS, QK_NOPE + V_HEAD)
    w_ukv_ext = jnp.concatenate([w_ukv[:, :, :QK_NOPE].reshape(KV_LORA, -1),
                                 w_ukv[:, :, QK_NOPE:].reshape(KV_LORA, -1)], axis=1).astype(BF16)
    tc, ts = _rope_tables(n, n_ctx)

    q, k, v = _mla_qkv(x, ctx, mod_qkv, norm_mix[0][None], mla_q_norm[0][None], mla_kv_norm[0][None],
                       tc, ts, w_down, w_uq_ext, w_ukv_ext)
    n_all = n + n_ctx
    tk = 768 if n_all % 768 == 0 else n_ctx
    o = _attention(q, k, v, tq=min(512, n), tk=tk)
    x = _ffn(x, o, m0[:, 2:6], norm_ffn[0][None], mla_w_o[0].astype(BF16), ffn_w_gate[0].astype(BF16),
             ffn_w_up[0].astype(BF16), ffn_w_down[0].astype(BF16), None, tm=min(256, n))

    x = _short_conv(x, m1[:, 0:3], norm_mix[1][None], conv_w_in[0].astype(BF16), conv_w[0],
                    conv_w_out[0].astype(BF16), tm=min(256, n))
    mod_ffn1 = jnp.concatenate([m1[:, 2:3], m1[:, 3:6]], axis=1)
    return _ffn(x, None, mod_ffn1, norm_ffn[1][None], None, ffn_w_gate[1].astype(BF16), ffn_w_up[1].astype(BF16),
                ffn_w_down[1].astype(BF16), final_norm[None], tm=min(256, n))
```

```python
import functools
import math

import numpy as np
import jax
import jax.numpy as jnp
from jax import lax
from jax.experimental import pallas as pl
from jax.experimental.pallas import tpu as pltpu

D_MODEL = 1024
GRID_W = 64
N_HEADS = 8
QK_NOPE = 128
QK_ROPE = 64
V_HEAD = 128
Q_LORA = 256
KV_LORA = 128
QK_DIM = QK_NOPE + QK_ROPE
V_ROWS = V_HEAD + 16
ROPE_THETA = 10000.0
NORM_EPS = 1e-6
ATTN_SCALE = 1.0 / math.sqrt(QK_NOPE + QK_ROPE)
LOG2_E = 1.4426950408889634

F32 = jnp.float32
BF16 = jnp.bfloat16

VMEM_LIMIT_BYTES = 56 * 1024 * 1024
HALO = 8


def _params(*sem):
    return pltpu.CompilerParams(dimension_semantics=sem, vmem_limit_bytes=VMEM_LIMIT_BYTES)


def _resident(shape):
    zeros = (0,) * len(shape)
    return pl.BlockSpec(shape, lambda *_: zeros, pipeline_mode=pl.Buffered(1))


def _rms(x, w):
    return x * lax.rsqrt(jnp.mean(x * x, axis=-1, keepdims=True) + NORM_EPS) * w


def _dot(a, b):
    return jnp.dot(a, b, preferred_element_type=F32)


def _dot_nt(a, b):
    return lax.dot_general(a, b, (((1,), (1,)), ((), ())), preferred_element_type=F32)


def _mod_kernel(c_ref, w_ref, b_ref, o_ref):
    c = c_ref[...]
    o_ref[0] = _dot(c * jax.nn.sigmoid(c), w_ref[0]) + b_ref[0]


def _modulation(c_rows, ada_w, ada_b):
    depth, d, d6 = ada_w.shape
    rows = c_rows.shape[0]
    tn = 1536
    return pl.pallas_call(
        _mod_kernel,
        grid=(depth, d6 // tn),
        in_specs=[
            pl.BlockSpec((rows, d), lambda i, n: (0, 0)),
            pl.BlockSpec((1, d, tn), lambda i, n: (i, 0, n)),
            pl.BlockSpec((1, 1, tn), lambda i, n: (i, 0, n)),
        ],
        out_specs=pl.BlockSpec((1, rows, tn), lambda i, n: (i, 0, n)),
        out_shape=jax.ShapeDtypeStruct((depth, rows, d6), F32),
        compiler_params=_params("arbitrary", "arbitrary"),
        name="adaln_modulation",
    )(c_rows, ada_w, ada_b.reshape(depth, 1, d6))


def _qkv_kernel(x_ref, ctx_ref, mod_ref, nw_ref, qn_ref, kvn_ref, tc_ref, ts_ref, tct_ref, tst_ref,
                wdn_ref, wuqt_ref, wuk_ref, wvt_ref, q_ref, k_ref, v_ref):
    j = pl.program_id(1)
    xin = jnp.where(j == 0, ctx_ref[0], x_ref[0])
    shift = mod_ref[0, 0, 0:1, :]
    scale = mod_ref[0, 0, 1:2, :]
    h = (_rms(xin, nw_ref[...]) * (1.0 + scale) + shift).astype(BF16)
    down = _dot(h, wdn_ref[...])

    ckv = _rms(down[:, Q_LORA:Q_LORA + KV_LORA], kvn_ref[...]).astype(BF16)
    kr0 = Q_LORA + KV_LORA
    k_rope = (down[:, kr0:kr0 + QK_ROPE] * tc_ref[...]
              + down[:, kr0 + QK_ROPE:kr0 + 2 * QK_ROPE] * ts_ref[...]).astype(BF16)
    k_nope = _dot(ckv, wuk_ref[...])
    v_t = _dot_nt(wvt_ref[...], ckv)
    ones = jnp.ones((V_ROWS - V_HEAD, v_t.shape[1]), BF16)
    for hd in range(N_HEADS):
        k_ref[0, hd, :, 0:QK_NOPE] = k_nope[:, hd * QK_NOPE:(hd + 1) * QK_NOPE].astype(BF16)
        k_ref[0, hd, :, QK_NOPE:QK_DIM] = k_rope
        v_ref[0, hd, 0:V_HEAD, :] = v_t[hd * V_HEAD:(hd + 1) * V_HEAD, :].astype(BF16)
        v_ref[0, hd, V_HEAD:V_ROWS, :] = ones

    @pl.when(j > 0)
    def _():
        cq = _rms(down[:, 0:Q_LORA], qn_ref[...]).astype(BF16)
        q_t = _dot_nt(wuqt_ref[...], cq) * (ATTN_SCALE * LOG2_E)
        r0 = N_HEADS * QK_NOPE
        r1 = r0 + N_HEADS * QK_ROPE
        for pair in range(N_HEADS // 2):
            lo = 2 * QK_ROPE * pair
            rope2 = (q_t[r0 + lo:r0 + lo + 2 * QK_ROPE, :] * tct_ref[...]
                     + q_t[r1 + lo:r1 + lo + 2 * QK_ROPE, :] * tst_ref[...]).astype(BF16)
            for sub in range(2):
                hd = 2 * pair + sub
                q_ref[0, hd, 0:QK_NOPE, :] = q_t[hd * QK_NOPE:(hd + 1) * QK_NOPE, :].astype(BF16)
                q_ref[0, hd, QK_NOPE:QK_DIM, :] = rope2[sub * QK_ROPE:(sub + 1) * QK_ROPE, :]


def _mla_qkv(x, ctx, mod, norm_w, q_norm, kv_norm, tc, ts, tct, tst, w_down, w_uq_t, w_uk, w_v_t):
    b, n, d = x.shape
    n_ctx = ctx.shape[1]
    tm = n_ctx
    assert n % tm == 0
    n_all = n_ctx + n
    grid = (b, n_all // tm)
    lat = lambda bi, j: (bi, jnp.maximum(j - 1, 0), 0)
    return pl.pallas_call(
        _qkv_kernel,
        grid=grid,
        in_specs=[
            pl.BlockSpec((1, tm, d), lat),
            pl.BlockSpec((1, n_ctx, d), lambda bi, j: (bi, 0, 0)),
            pl.BlockSpec((1, 1, 2, d), lambda bi, j: (bi, jnp.minimum(j, 1), 0, 0)),
            _resident((1, d)),
            _resident((1, Q_LORA)),
            _resident((1, KV_LORA)),
            pl.BlockSpec((tm, QK_ROPE), lambda bi, j: (j, 0)),
            pl.BlockSpec((tm, QK_ROPE), lambda bi, j: (j, 0)),
            pl.BlockSpec((2 * QK_ROPE, tm), lambda bi, j: (0, jnp.maximum(j - 1, 0))),
            pl.BlockSpec((2 * QK_ROPE, tm), lambda bi, j: (0, jnp.maximum(j - 1, 0))),
            _resident(w_down.shape),
            _resident(w_uq_t.shape),
            _resident(w_uk.shape),
            _resident(w_v_t.shape),
        ],
        out_specs=[
            pl.BlockSpec((1, N_HEADS, QK_DIM, tm), lambda bi, j: (bi, 0, 0, jnp.maximum(j - 1, 0))),
            pl.BlockSpec((1, N_HEADS, tm, QK_DIM), lambda bi, j: (bi, 0, j, 0)),
            pl.BlockSpec((1, N_HEADS, V_ROWS, tm), lambda bi, j: (bi, 0, 0, j)),
        ],
        out_shape=[
            jax.ShapeDtypeStruct((b, N_HEADS, QK_DIM, n), BF16),
            jax.ShapeDtypeStruct((b, N_HEADS, n_all, QK_DIM), BF16),
            jax.ShapeDtypeStruct((b, N_HEADS, V_ROWS, n_all), BF16),
        ],
        compiler_params=_params("arbitrary", "arbitrary"),
        name="mla_qkv",
    )(x, ctx, mod, norm_w, q_norm, kv_norm, tc, ts, tct, tst, w_down, w_uq_t, w_uk, w_v_t)


def _attn_kernel(q_ref, k_ref, v_ref, o_ref, *, tk):
    q_t = q_ref[0, 0]
    tq = q_t.shape[1]
    n_keys = k_ref.shape[2]
    n_chunks = n_keys // tk

    def scores(c):
        return _dot(k_ref[0, 0, c * tk:(c + 1) * tk, :], q_t)

    def softmax_step(s, m):
        m_new = jnp.maximum(m, jnp.max(s, axis=0, keepdims=True))
        return jnp.exp2(s - m_new).astype(BF16), jnp.exp2(m - m_new), m_new

    s_cur = scores(0)
    s_nxt = scores(1) if n_chunks > 1 else None
    p, alpha, m = softmax_step(s_cur, jnp.full((1, tq), -jnp.inf, F32))
    acc = jnp.zeros((V_ROWS, tq), F32)
    for c in range(n_chunks):
        s_cur = s_nxt
        if c + 2 < n_chunks:
            s_nxt = scores(c + 2)
        p_now, alpha_now = p, alpha
        if c + 1 < n_chunks:
            p, alpha, m = softmax_step(s_cur, m)
        acc = alpha_now * acc + _dot(v_ref[0, 0, :, c * tk:(c + 1) * tk], p_now)
    o_t = acc[0:V_HEAD, :] / acc[V_HEAD:V_HEAD + 1, :]
    o_ref[0] = o_t.T.astype(BF16)


def _attention(q_t, k, v_t, tq, tk):
    b, nh, _, n = q_t.shape
    n_all = k.shape[2]
    assert n % tq == 0 and n_all % tk == 0
    return pl.pallas_call(
        functools.partial(_attn_kernel, tk=tk),
        grid=(b, nh, n // tq),
        in_specs=[
            pl.BlockSpec((1, 1, QK_DIM, tq), lambda bi, hi, i: (bi, hi, 0, i)),
            pl.BlockSpec((1, 1, n_all, QK_DIM), lambda bi, hi, i: (bi, hi, 0, 0)),
            pl.BlockSpec((1, 1, V_ROWS, n_all), lambda bi, hi, i: (bi, hi, 0, 0)),
        ],
        out_specs=pl.BlockSpec((1, tq, V_HEAD), lambda bi, hi, i: (bi, i, hi)),
        out_shape=jax.ShapeDtypeStruct((b, n, nh * V_HEAD), BF16),
        compiler_params=_params("arbitrary", "arbitrary", "arbitrary"),
        name="mla_attention",
    )(q_t, k, v_t)


def _ffn_kernel(*refs, with_proj, with_final):
    refs = list(refs)
    x_ref = refs.pop(0)
    o_ref = refs.pop(0) if with_proj else None
    mod_ref = refs.pop(0)
    nw_ref = refs.pop(0)
    wo_ref = refs.pop(0) if with_proj else None
    wg_ref, wu_ref, wd_ref = refs.pop(0), refs.pop(0), refs.pop(0)
    fw_ref = refs.pop(0) if with_final else None
    out_ref = refs.pop(0)

    x = x_ref[0]
    if with_proj:
        x = x + mod_ref[0, 0:1, :] * _dot(o_ref[0], wo_ref[...])
    shift, scale, gate = mod_ref[0, 1:2, :], mod_ref[0, 2:3, :], mod_ref[0, 3:4, :]
    h = (_rms(x, nw_ref[...]) * (1.0 + scale) + shift).astype(BF16)
    g = _dot(h, wg_ref[...])
    u = _dot(h, wu_ref[...])
    act = (g * jax.nn.sigmoid(g) * u).astype(BF16)
    x = x + gate * _dot(act, wd_ref[...])
    if with_final:
        x = _rms(x, fw_ref[...])
    out_ref[0] = x


def _ffn(x, o, mod, norm_w, w_o, w_gate, w_up, w_down, final_w, tm):
    b, n, d = x.shape
    with_proj = o is not None
    with_final = final_w is not None
    tile = pl.BlockSpec((1, tm, d), lambda bi, i: (bi, i, 0))
    args, specs = [x], [tile]
    if with_proj:
        args.append(o)
        specs.append(tile)
    args += [mod, norm_w]
    specs += [pl.BlockSpec((1, 4, d), lambda bi, i: (bi, 0, 0)), _resident((1, d))]
    if with_proj:
        args.append(w_o)
        specs.append(_resident(w_o.shape))
    args += [w_gate, w_up, w_down]
    specs += [_resident(w_gate.shape), _resident(w_up.shape), _resident(w_down.shape)]
    if with_final:
        args.append(final_w)
        specs.append(_resident((1, d)))
    return pl.pallas_call(
        functools.partial(_ffn_kernel, with_proj=with_proj, with_final=with_final),
        grid=(b, n // tm),
        in_specs=specs,
        out_specs=tile,
        out_shape=jax.ShapeDtypeStruct((b, n, d), F32),
        compiler_params=_params("arbitrary", "arbitrary"),
        name="attn_proj_ffn" if with_proj else "ffn_final",
    )(*args)


def _conv_kernel(x_ref, xp_ref, xn_ref, mod_ref, nw_ref, win_ref, cw_ref, wout_ref, out_ref, z_ref):
    i = pl.program_id(1)
    last = pl.num_programs(1) - 1
    d = x_ref.shape[2]
    tm = x_ref.shape[1]
    x = x_ref[0]
    xe = jnp.concatenate([xp_ref[0], x, xn_ref[0]], axis=0)
    shift, scale, gate = mod_ref[0, 0:1, :], mod_ref[0, 1:2, :], mod_ref[0, 2:3, :]
    h = (_rms(xe, nw_ref[...]) * (1.0 + scale) + shift).astype(BF16)
    p = _dot(h, win_ref[...])
    z = p[:, d:2 * d] * p[:, 2 * d:3 * d]
    row = lax.broadcasted_iota(jnp.int32, (tm + 2 * HALO, 1), 0)
    outside = ((row < HALO) & (i == 0)) | ((row >= tm + HALO) & (i == last))
    z_ref[...] = jnp.where(outside, 0.0, z)
    u = (cw_ref[0:1, :] * z_ref[HALO - 1:HALO - 1 + tm, :]
         + cw_ref[1:2, :] * z_ref[HALO:HALO + tm, :]
         + cw_ref[2:3, :] * z_ref[HALO + 1:HALO + 1 + tm, :])
    y = _dot((p[HALO:HALO + tm, 0:d] * u).astype(BF16), wout_ref[...])
    out_ref[0] = x + gate * y


def _short_conv(x, mod, norm_w, w_in, conv_w, w_out, tm):
    b, n, d = x.shape
    nblk = n // HALO
    per = tm // HALO
    return pl.pallas_call(
        _conv_kernel,
        grid=(b, n // tm),
        in_specs=[
            pl.BlockSpec((1, tm, d), lambda bi, i: (bi, i, 0)),
            pl.BlockSpec((1, HALO, d), lambda bi, i: (bi, jnp.maximum(i * per - 1, 0), 0)),
            pl.BlockSpec((1, HALO, d), lambda bi, i: (bi, jnp.minimum((i + 1) * per, nblk - 1), 0)),
            pl.BlockSpec((1, 3, d), lambda bi, i: (bi, 0, 0)),
            _resident((1, d)),
            _resident(w_in.shape),
            _resident(conv_w.shape),
            _resident(w_out.shape),
        ],
        out_specs=pl.BlockSpec((1, tm, d), lambda bi, i: (bi, i, 0)),
        out_shape=jax.ShapeDtypeStruct((b, n, d), F32),
        scratch_shapes=[pltpu.VMEM((tm + 2 * HALO, d), F32)],
        compiler_params=_params("arbitrary", "arbitrary"),
        name="short_conv",
    )(x, x, x, mod, norm_w, w_in, conv_w, w_out)


def _rope_tables(n):
    rows = n // GRID_W
    row, col = jnp.meshgrid(jnp.arange(rows), jnp.arange(GRID_W), indexing="ij")
    row = row.reshape(-1).astype(F32)
    col = col.reshape(-1).astype(F32)
    n_freq = QK_ROPE // 4
    freqs = ROPE_THETA ** (-jnp.arange(n_freq, dtype=F32) / n_freq)
    ar, ac = row[:, None] * freqs, col[:, None] * freqs
    cos = jnp.concatenate([jnp.cos(ar), jnp.cos(ar), jnp.cos(ac), jnp.cos(ac)], axis=-1)
    sin = jnp.concatenate([jnp.sin(ar), jnp.sin(ar), jnp.sin(ac), jnp.sin(ac)], axis=-1)
    return cos, sin


def _rotate_cols(w):
    q = QK_ROPE // 4
    perm = np.concatenate([np.arange(q, 2 * q), np.arange(0, q), np.arange(3 * q, 4 * q), np.arange(2 * q, 3 * q)])
    sign = np.concatenate([-np.ones(q), np.ones(q), -np.ones(q), np.ones(q)]).astype(np.float32)
    return w[..., perm] * sign


def kernel(x, c, ctx, c_ctx, ada_w, ada_b, norm_mix, norm_ffn, mla_w_dq, mla_q_norm, mla_w_uq, mla_w_dkv,
           mla_kv_norm, mla_w_ukv, mla_w_o, conv_w_in, conv_w, conv_w_out, ffn_w_gate, ffn_w_up, ffn_w_down,
           final_norm):
    b, n, d = x.shape
    n_ctx = ctx.shape[1]

    pad = (-(b + 1)) % 8
    c_rows = jnp.concatenate([c, c_ctx[None, :], jnp.zeros((pad, d), F32)], axis=0)
    mods = _modulation(c_rows, ada_w, ada_b)
    m0 = mods[0, :b].reshape(b, 6, d)
    m1 = mods[1, :b].reshape(b, 6, d)
    mc = jnp.broadcast_to(mods[0, b].reshape(1, 6, d)[:, 0:2], (b, 2, d))
    mod_qkv = jnp.stack([mc, m0[:, 0:2]], axis=1)

    w_uq = mla_w_uq[0].reshape(Q_LORA, N_HEADS, QK_DIM)
    uq_rope = w_uq[:, :, QK_NOPE:]
    w_uq_t = jnp.concatenate([w_uq[:, :, :QK_NOPE].reshape(Q_LORA, -1), uq_rope.reshape(Q_LORA, -1),
                              _rotate_cols(uq_rope).reshape(Q_LORA, -1)], axis=1).T.astype(BF16)
    w_dkv = mla_w_dkv[0]
    w_down = jnp.concatenate([mla_w_dq[0], w_dkv, _rotate_cols(w_dkv[:, KV_LORA:])], axis=1).astype(BF16)
    w_ukv = mla_w_ukv[0].reshape(KV_LORA, N_HEADS, QK_NOPE + V_HEAD)
    w_uk = w_ukv[:, :, :QK_NOPE].reshape(KV_LORA, -1).astype(BF16)
    w_v_t = w_ukv[:, :, QK_NOPE:].reshape(KV_LORA, -1).T.astype(BF16)
    cos, sin = _rope_tables(n)
    tc = jnp.concatenate([jnp.ones((n_ctx, QK_ROPE), F32), cos], axis=0)
    ts = jnp.concatenate([jnp.zeros((n_ctx, QK_ROPE), F32), sin], axis=0)
    tct = jnp.tile(cos.T, (2, 1))
    tst = jnp.tile(sin.T, (2, 1))

    q_t, k, v_t = _mla_qkv(x, ctx, mod_qkv, norm_mix[0][None], mla_q_norm[0][None], mla_kv_norm[0][None],
                           tc, ts, tct, tst, w_down, w_uq_t, w_uk, w_v_t)
    n_all = n + n_ctx
    tk = 768 if n_all % 768 == 0 else n_ctx
    o = _attention(q_t, k, v_t, tq=min(1024, n), tk=tk)
    x = _ffn(x, o, m0[:, 2:6], norm_ffn[0][None], mla_w_o[0].astype(BF16), ffn_w_gate[0].astype(BF16),
             ffn_w_up[0].astype(BF16), ffn_w_down[0].astype(BF16), None, tm=min(256, n))

    x = _short_conv(x, m1[:, 0:3], norm_mix[1][None], conv_w_in[0].astype(BF16), conv_w[0],
                    conv_w_out[0].astype(BF16), tm=min(256, n))
    return _ffn(x, None, m1[:, 2:6], norm_ffn[1][None], None, ffn_w_gate[1].astype(BF16), ffn_w_up[1].astype(BF16),
                ffn_w_down[1].astype(BF16), final_norm[None], tm=min(256, n))
```

```python
import functools
import math

import numpy as np
import jax
import jax.numpy as jnp
from jax import lax
from jax.experimental import pallas as pl
from jax.experimental.pallas import tpu as pltpu

D_MODEL = 1024
GRID_W = 64
N_HEADS = 8
QK_NOPE = 128
QK_ROPE = 64
V_HEAD = 128
Q_LORA = 256
KV_LORA = 128
QK_DIM = QK_NOPE + QK_ROPE
V_ROWS = V_HEAD + 16
ROPE_THETA = 10000.0
NORM_EPS = 1e-6
ATTN_SCALE = 1.0 / math.sqrt(QK_NOPE + QK_ROPE)
LOG2_E = 1.4426950408889634

F32 = jnp.float32
BF16 = jnp.bfloat16

VMEM_LIMIT_BYTES = 56 * 1024 * 1024
HALO = 8


def _params(*sem):
    return pltpu.CompilerParams(dimension_semantics=sem, vmem_limit_bytes=VMEM_LIMIT_BYTES)


def _resident(shape):
    zeros = (0,) * len(shape)
    return pl.BlockSpec(shape, lambda *_: zeros, pipeline_mode=pl.Buffered(1))


def _rms(x, w):
    return x * lax.rsqrt(jnp.mean(x * x, axis=-1, keepdims=True) + NORM_EPS) * w


def _dot(a, b):
    return jnp.dot(a, b, preferred_element_type=F32)


def _dot_nt(a, b):
    return lax.dot_general(a, b, (((1,), (1,)), ((), ())), preferred_element_type=F32)


def _mod_kernel(c_ref, w_ref, b_ref, o_ref):
    c = c_ref[...]
    o_ref[0] = _dot(c * jax.nn.sigmoid(c), w_ref[0]) + b_ref[0]


def _modulation(c_rows, ada_w, ada_b):
    depth, d, d6 = ada_w.shape
    rows = c_rows.shape[0]
    tn = 1536
    return pl.pallas_call(
        _mod_kernel,
        grid=(depth, d6 // tn),
        in_specs=[
            pl.BlockSpec((rows, d), lambda i, n: (0, 0)),
            pl.BlockSpec((1, d, tn), lambda i, n: (i, 0, n)),
            pl.BlockSpec((1, 1, tn), lambda i, n: (i, 0, n)),
        ],
        out_specs=pl.BlockSpec((1, rows, tn), lambda i, n: (i, 0, n)),
        out_shape=jax.ShapeDtypeStruct((depth, rows, d6), F32),
        compiler_params=_params("arbitrary", "arbitrary"),
        name="adaln_modulation",
    )(c_rows, ada_w, ada_b.reshape(depth, 1, d6))


def _kv_store(ckv, k_rope, wuk_ref, wvt_ref, k_ref, v_ref):
    k_nope = _dot(ckv, wuk_ref[...])
    v_t = _dot_nt(wvt_ref[...], ckv)
    ones = jnp.ones((V_ROWS - V_HEAD, v_t.shape[1]), BF16)
    for hd in range(N_HEADS):
        k_ref[0, hd, :, 0:QK_NOPE] = k_nope[:, hd * QK_NOPE:(hd + 1) * QK_NOPE].astype(BF16)
        k_ref[0, hd, :, QK_NOPE:QK_DIM] = k_rope
        v_ref[0, hd, 0:V_HEAD, :] = v_t[hd * V_HEAD:(hd + 1) * V_HEAD, :].astype(BF16)
        v_ref[0, hd, V_HEAD:V_ROWS, :] = ones


def _qkv_latent_kernel(x_ref, mod_ref, nw_ref, qn_ref, kvn_ref, tc_ref, ts_ref, tct_ref, tst_ref,
                       wdn_ref, wuqt_ref, wuk_ref, wvt_ref, q_ref, k_ref, v_ref):
    shift, scale = mod_ref[0, 0:1, :], mod_ref[0, 1:2, :]
    h = (_rms(x_ref[0], nw_ref[...]) * (1.0 + scale) + shift).astype(BF16)
    down = _dot(h, wdn_ref[...])
    ckv = _rms(down[:, Q_LORA:Q_LORA + KV_LORA], kvn_ref[...]).astype(BF16)
    kr0 = Q_LORA + KV_LORA
    k_rope = (down[:, kr0:kr0 + QK_ROPE] * tc_ref[...]
              + down[:, kr0 + QK_ROPE:kr0 + 2 * QK_ROPE] * ts_ref[...]).astype(BF16)
    _kv_store(ckv, k_rope, wuk_ref, wvt_ref, k_ref, v_ref)

    cq = _rms(down[:, 0:Q_LORA], qn_ref[...]).astype(BF16)
    q_t = _dot_nt(wuqt_ref[...], cq) * (ATTN_SCALE * LOG2_E)
    r0 = N_HEADS * QK_NOPE
    r1 = r0 + N_HEADS * QK_ROPE
    for pair in range(N_HEADS // 2):
        lo = 2 * QK_ROPE * pair
        rope2 = (q_t[r0 + lo:r0 + lo + 2 * QK_ROPE, :] * tct_ref[...]
                 + q_t[r1 + lo:r1 + lo + 2 * QK_ROPE, :] * tst_ref[...]).astype(BF16)
        for sub in range(2):
            hd = 2 * pair + sub
            q_ref[0, hd, 0:QK_NOPE, :] = q_t[hd * QK_NOPE:(hd + 1) * QK_NOPE, :].astype(BF16)
            q_ref[0, hd, QK_NOPE:QK_DIM, :] = rope2[sub * QK_ROPE:(sub + 1) * QK_ROPE, :]


def _kv_context_kernel(x_ref, mod_ref, nw_ref, kvn_ref, wdn_ref, wuk_ref, wvt_ref, k_in, v_in, k_ref, v_ref):
    del k_in, v_in
    shift, scale = mod_ref[0:1, :], mod_ref[1:2, :]
    h = (_rms(x_ref[0], nw_ref[...]) * (1.0 + scale) + shift).astype(BF16)
    down = _dot(h, wdn_ref[...])
    ckv = _rms(down[:, Q_LORA:Q_LORA + KV_LORA], kvn_ref[...]).astype(BF16)
    kr0 = Q_LORA + KV_LORA
    k_rope = down[:, kr0:kr0 + QK_ROPE].astype(BF16)
    _kv_store(ckv, k_rope, wuk_ref, wvt_ref, k_ref, v_ref)


def _mla_qkv(x, ctx, mod, mod_ctx, norm_w, q_norm, kv_norm, tc, ts, tct, tst, w_down, w_uq_t, w_uk, w_v_t, tm):
    b, n, d = x.shape
    n_ctx = ctx.shape[1]
    n_all = n_ctx + n
    assert n % tm == 0 and n % n_ctx == 0
    k_shape = jax.ShapeDtypeStruct((b, N_HEADS, n_all, QK_DIM), BF16)
    v_shape = jax.ShapeDtypeStruct((b, N_HEADS, V_ROWS, n_all), BF16)
    q_t, k, v_t = pl.pallas_call(
        _qkv_latent_kernel,
        grid=(b, n // tm),
        in_specs=[
            pl.BlockSpec((1, tm, d), lambda bi, j: (bi, j, 0)),
            pl.BlockSpec((1, 2, d), lambda bi, j: (bi, 0, 0)),
            _resident((1, d)),
            _resident((1, Q_LORA)),
            _resident((1, KV_LORA)),
            pl.BlockSpec((tm, QK_ROPE), lambda bi, j: (j, 0)),
            pl.BlockSpec((tm, QK_ROPE), lambda bi, j: (j, 0)),
            pl.BlockSpec((2 * QK_ROPE, tm), lambda bi, j: (0, j)),
            pl.BlockSpec((2 * QK_ROPE, tm), lambda bi, j: (0, j)),
            _resident(w_down.shape),
            _resident(w_uq_t.shape),
            _resident(w_uk.shape),
            _resident(w_v_t.shape),
        ],
        out_specs=[
            pl.BlockSpec((1, N_HEADS, QK_DIM, tm), lambda bi, j: (bi, 0, 0, j)),
            pl.BlockSpec((1, N_HEADS, tm, QK_DIM), lambda bi, j: (bi, 0, j, 0)),
            pl.BlockSpec((1, N_HEADS, V_ROWS, tm), lambda bi, j: (bi, 0, 0, j)),
        ],
        out_shape=[jax.ShapeDtypeStruct((b, N_HEADS, QK_DIM, n), BF16), k_shape, v_shape],
        compiler_params=_params("arbitrary", "arbitrary"),
        name="mla_qkv_latent",
    )(x, mod, norm_w, q_norm, kv_norm, tc, ts, tct, tst, w_down, w_uq_t, w_uk, w_v_t)
    last = n // n_ctx
    k, v_t = pl.pallas_call(
        _kv_context_kernel,
        grid=(b,),
        in_specs=[
            pl.BlockSpec((1, n_ctx, d), lambda bi: (bi, 0, 0)),
            _resident((2, d)),
            _resident((1, d)),
            _resident((1, KV_LORA)),
            _resident(w_down.shape),
            _resident(w_uk.shape),
            _resident(w_v_t.shape),
            pl.BlockSpec(memory_space=pl.ANY),
            pl.BlockSpec(memory_space=pl.ANY),
        ],
        out_specs=[
            pl.BlockSpec((1, N_HEADS, n_ctx, QK_DIM), lambda bi: (bi, 0, last, 0)),
            pl.BlockSpec((1, N_HEADS, V_ROWS, n_ctx), lambda bi: (bi, 0, 0, last)),
        ],
        out_shape=[k_shape, v_shape],
        input_output_aliases={7: 0, 8: 1},
        compiler_params=_params("arbitrary"),
        name="mla_kv_context",
    )(ctx, mod_ctx, norm_w, kv_norm, w_down, w_uk, w_v_t, k, v_t)
    return q_t, k, v_t


def _attn_kernel(q_ref, k_ref, v_ref, o_ref, *, tk):
    q_t = q_ref[0, 0]
    tq = q_t.shape[1]
    n_keys = k_ref.shape[2]
    n_chunks = n_keys // tk

    def scores(c):
        return _dot(k_ref[0, 0, c * tk:(c + 1) * tk, :], q_t)

    def softmax_step(s, m):
        m_new = jnp.maximum(m, jnp.max(s, axis=0, keepdims=True))
        return jnp.exp2(s - m_new).astype(BF16), jnp.exp2(m - m_new), m_new

    s_cur = scores(0)
    s_nxt = scores(1) if n_chunks > 1 else None
    p, alpha, m = softmax_step(s_cur, jnp.full((1, tq), -jnp.inf, F32))
    acc = jnp.zeros((V_ROWS, tq), F32)
    for c in range(n_chunks):
        s_cur = s_nxt
        if c + 2 < n_chunks:
            s_nxt = scores(c + 2)
        p_now, alpha_now = p, alpha
        if c + 1 < n_chunks:
            p, alpha, m = softmax_step(s_cur, m)
        acc = alpha_now * acc + _dot(v_ref[0, 0, :, c * tk:(c + 1) * tk], p_now)
    o_t = acc[0:V_HEAD, :] / acc[V_HEAD:V_HEAD + 1, :]
    o_ref[0] = o_t.T.astype(BF16)


def _attention(q_t, k, v_t, tq, tk):
    b, nh, _, n = q_t.shape
    n_all = k.shape[2]
    assert n % tq == 0 and n_all % tk == 0
    return pl.pallas_call(
        functools.partial(_attn_kernel, tk=tk),
        grid=(b, nh, n // tq),
        in_specs=[
            pl.BlockSpec((1, 1, QK_DIM, tq), lambda bi, hi, i: (bi, hi, 0, i)),
            pl.BlockSpec((1, 1, n_all, QK_DIM), lambda bi, hi, i: (bi, hi, 0, 0)),
            pl.BlockSpec((1, 1, V_ROWS, n_all), lambda bi, hi, i: (bi, hi, 0, 0)),
        ],
        out_specs=pl.BlockSpec((1, tq, V_HEAD), lambda bi, hi, i: (bi, i, hi)),
        out_shape=jax.ShapeDtypeStruct((b, n, nh * V_HEAD), BF16),
        compiler_params=_params("arbitrary", "arbitrary", "arbitrary"),
        name="mla_attention",
    )(q_t, k, v_t)


def _ffn_kernel(*refs, with_proj, with_final):
    refs = list(refs)
    x_ref = refs.pop(0)
    o_ref = refs.pop(0) if with_proj else None
    mod_ref = refs.pop(0)
    nw_ref = refs.pop(0)
    wo_ref = refs.pop(0) if with_proj else None
    wg_ref, wu_ref, wd_ref = refs.pop(0), refs.pop(0), refs.pop(0)
    fw_ref = refs.pop(0) if with_final else None
    out_ref = refs.pop(0)

    x = x_ref[0]
    if with_proj:
        x = x + mod_ref[0, 0:1, :] * _dot(o_ref[0], wo_ref[...])
    shift, scale, gate = mod_ref[0, 1:2, :], mod_ref[0, 2:3, :], mod_ref[0, 3:4, :]
    h = (_rms(x, nw_ref[...]) * (1.0 + scale) + shift).astype(BF16)
    g = _dot(h, wg_ref[...])
    u = _dot(h, wu_ref[...])
    act = (g * jax.nn.sigmoid(g) * u).astype(BF16)
    x = x + gate * _dot(act, wd_ref[...])
    if with_final:
        x = _rms(x, fw_ref[...])
    out_ref[0] = x


def _ffn(x, o, mod, norm_w, w_o, w_gate, w_up, w_down, final_w, tm):
    b, n, d = x.shape
    with_proj = o is not None
    with_final = final_w is not None
    tile = pl.BlockSpec((1, tm, d), lambda bi, i: (bi, i, 0))
    args, specs = [x], [tile]
    if with_proj:
        args.append(o)
        specs.append(tile)
    args += [mod, norm_w]
    specs += [pl.BlockSpec((1, 4, d), lambda bi, i: (bi, 0, 0)), _resident((1, d))]
    if with_proj:
        args.append(w_o)
        specs.append(_resident(w_o.shape))
    args += [w_gate, w_up, w_down]
    specs += [_resident(w_gate.shape), _resident(w_up.shape), _resident(w_down.shape)]
    if with_final:
        args.append(final_w)
        specs.append(_resident((1, d)))
    return pl.pallas_call(
        functools.partial(_ffn_kernel, with_proj=with_proj, with_final=with_final),
        grid=(b, n // tm),
        in_specs=specs,
        out_specs=tile,
        out_shape=jax.ShapeDtypeStruct((b, n, d), F32),
        compiler_params=_params("arbitrary", "arbitrary"),
        name="attn_proj_ffn" if with_proj else "ffn_final",
    )(*args)


def _conv_kernel(x_ref, xp_ref, xn_ref, mod_ref, nw_ref, win_ref, cw_ref, wout_ref, out_ref, z_ref):
    i = pl.program_id(1)
    last = pl.num_programs(1) - 1
    d = x_ref.shape[2]
    tm = x_ref.shape[1]
    x = x_ref[0]
    xe = jnp.concatenate([xp_ref[0], x, xn_ref[0]], axis=0)
    shift, scale, gate = mod_ref[0, 0:1, :], mod_ref[0, 1:2, :], mod_ref[0, 2:3, :]
    h = (_rms(xe, nw_ref[...]) * (1.0 + scale) + shift).astype(BF16)
    p = _dot(h, win_ref[...])
    z = p[:, d:2 * d] * p[:, 2 * d:3 * d]
    row = lax.broadcasted_iota(jnp.int32, (tm + 2 * HALO, 1), 0)
    outside = ((row < HALO) & (i == 0)) | ((row >= tm + HALO) & (i == last))
    z_ref[...] = jnp.where(outside, 0.0, z)
    u = (cw_ref[0:1, :] * z_ref[HALO - 1:HALO - 1 + tm, :]
         + cw_ref[1:2, :] * z_ref[HALO:HALO + tm, :]
         + cw_ref[2:3, :] * z_ref[HALO + 1:HALO + 1 + tm, :])
    y = _dot((p[HALO:HALO + tm, 0:d] * u).astype(BF16), wout_ref[...])
    out_ref[0] = x + gate * y


def _short_conv(x, mod, norm_w, w_in, conv_w, w_out, tm):
    b, n, d = x.shape
    nblk = n // HALO
    per = tm // HALO
    return pl.pallas_call(
        _conv_kernel,
        grid=(b, n // tm),
        in_specs=[
            pl.BlockSpec((1, tm, d), lambda bi, i: (bi, i, 0)),
            pl.BlockSpec((1, HALO, d), lambda bi, i: (bi, jnp.maximum(i * per - 1, 0), 0)),
            pl.BlockSpec((1, HALO, d), lambda bi, i: (bi, jnp.minimum((i + 1) * per, nblk - 1), 0)),
            pl.BlockSpec((1, 3, d), lambda bi, i: (bi, 0, 0)),
            _resident((1, d)),
            _resident(w_in.shape),
            _resident(conv_w.shape),
            _resident(w_out.shape),
        ],
        out_specs=pl.BlockSpec((1, tm, d), lambda bi, i: (bi, i, 0)),
        out_shape=jax.ShapeDtypeStruct((b, n, d), F32),
        scratch_shapes=[pltpu.VMEM((tm + 2 * HALO, d), F32)],
        compiler_params=_params("arbitrary", "arbitrary"),
        name="short_conv",
    )(x, x, x, mod, norm_w, w_in, conv_w, w_out)


def _rope_tables(n):
    rows = n // GRID_W
    n_freq = QK_ROPE // 4
    freqs = ROPE_THETA ** (-jnp.arange(n_freq, dtype=F32) / n_freq)
    ar = jnp.arange(rows).astype(F32)[:, None] * freqs
    ac = jnp.arange(GRID_W).astype(F32)[:, None] * freqs

    def table(fn):
        r = jnp.broadcast_to(fn(ar)[:, None, :], (rows, GRID_W, n_freq)).reshape(n, n_freq)
        c = jnp.broadcast_to(fn(ac)[None, :, :], (rows, GRID_W, n_freq)).reshape(n, n_freq)
        return jnp.concatenate([r, r, c, c], axis=-1)

    return table(jnp.cos), table(jnp.sin)


def _rotate_cols(w):
    q = QK_ROPE // 4
    perm = np.concatenate([np.arange(q, 2 * q), np.arange(0, q), np.arange(3 * q, 4 * q), np.arange(2 * q, 3 * q)])
    sign = np.concatenate([-np.ones(q), np.ones(q), -np.ones(q), np.ones(q)]).astype(np.float32)
    return w[..., perm] * sign


def kernel(x, c, ctx, c_ctx, ada_w, ada_b, norm_mix, norm_ffn, mla_w_dq, mla_q_norm, mla_w_uq, mla_w_dkv,
           mla_kv_norm, mla_w_ukv, mla_w_o, conv_w_in, conv_w, conv_w_out, ffn_w_gate, ffn_w_up, ffn_w_down,
           final_norm):
    b, n, d = x.shape
    n_ctx = ctx.shape[1]

    pad = (-(b + 1)) % 8
    c_rows = jnp.concatenate([c, c_ctx[None, :], jnp.zeros((pad, d), F32)], axis=0)
    mods = _modulation(c_rows, ada_w, ada_b)
    m0 = mods[0, :b].reshape(b, 6, d)
    m1 = mods[1, :b].reshape(b, 6, d)
    mod_ctx = mods[0, b].reshape(6, d)[0:2]

    w_uq = mla_w_uq[0].reshape(Q_LORA, N_HEADS, QK_DIM)
    uq_rope = w_uq[:, :, QK_NOPE:]
    w_uq_t = jnp.concatenate([w_uq[:, :, :QK_NOPE].reshape(Q_LORA, -1), uq_rope.reshape(Q_LORA, -1),
                              _rotate_cols(uq_rope).reshape(Q_LORA, -1)], axis=1).T.astype(BF16)
    w_dkv = mla_w_dkv[0]
    w_down = jnp.concatenate([mla_w_dq[0], w_dkv, _rotate_cols(w_dkv[:, KV_LORA:])], axis=1).astype(BF16)
    w_ukv = mla_w_ukv[0].reshape(KV_LORA, N_HEADS, QK_NOPE + V_HEAD)
    w_uk = w_ukv[:, :, :QK_NOPE].reshape(KV_LORA, -1).astype(BF16)
    w_v_t = w_ukv[:, :, QK_NOPE:].reshape(KV_LORA, -1).T.astype(BF16)
    tc, ts = _rope_tables(n)
    tct = jnp.tile(tc.T, (2, 1))
    tst = jnp.tile(ts.T, (2, 1))

    q_t, k, v_t = _mla_qkv(x, ctx, m0[:, 0:2], mod_ctx, norm_mix[0][None], mla_q_norm[0][None],
                           mla_kv_norm[0][None], tc, ts, tct, tst, w_down, w_uq_t, w_uk, w_v_t, tm=min(512, n))
    n_all = n + n_ctx
    tk = 768 if n_all % 768 == 0 else n_ctx
    o = _attention(q_t, k, v_t, tq=min(1024, n), tk=tk)
    x = _ffn(x, o, m0[:, 2:6], norm_ffn[0][None], mla_w_o[0].astype(BF16), ffn_w_gate[0].astype(BF16),
             ffn_w_up[0].astype(BF16), ffn_w_down[0].astype(BF16), None, tm=min(512, n))

    x = _short_conv(x, m1[:, 0:3], norm_mix[1][None], conv_w_in[0].astype(BF16), conv_w[0],
                    conv_w_out[0].astype(BF16), tm=min(512, n))
    return _ffn(x, None, m1[:, 2:6], norm_ffn[1][None], None, ffn_w_gate[1].astype(BF16), ffn_w_up[1].astype(BF16),
                ffn_w_down[1].astype(BF16), final_norm[None], tm=min(512, n))
```

```python
import functools
import math

import numpy as np
import jax
import jax.numpy as jnp
from jax import lax
from jax.experimental import pallas as pl
from jax.experimental.pallas import tpu as pltpu

D_MODEL = 1024
GRID_W = 64
N_HEADS = 8
QK_NOPE = 128
QK_ROPE = 64
V_HEAD = 128
Q_LORA = 256
KV_LORA = 128
QK_DIM = QK_NOPE + QK_ROPE
V_ROWS = V_HEAD + 16
ROPE_THETA = 10000.0
NORM_EPS = 1e-6
ATTN_SCALE = 1.0 / math.sqrt(QK_NOPE + QK_ROPE)
LOG2_E = 1.4426950408889634
LAG_LIMIT = 60.0

F32 = jnp.float32
BF16 = jnp.bfloat16

VMEM_LIMIT_BYTES = 56 * 1024 * 1024
HALO = 8


def _params(*sem):
    return pltpu.CompilerParams(dimension_semantics=sem, vmem_limit_bytes=VMEM_LIMIT_BYTES)


def _resident(shape):
    zeros = (0,) * len(shape)
    return pl.BlockSpec(shape, lambda *_: zeros, pipeline_mode=pl.Buffered(1))


def _rms(x, w):
    return x * lax.rsqrt(jnp.mean(x * x, axis=-1, keepdims=True) + NORM_EPS) * w


def _dot(a, b):
    return jnp.dot(a, b, preferred_element_type=F32)


def _dot_nt(a, b):
    return lax.dot_general(a, b, (((1,), (1,)), ((), ())), preferred_element_type=F32)


def _mod_kernel(c_ref, w_ref, b_ref, o_ref):
    c = c_ref[...]
    o_ref[0] = _dot(c * jax.nn.sigmoid(c), w_ref[0]) + b_ref[0]


def _modulation(c_rows, ada_w, ada_b):
    depth, d, d6 = ada_w.shape
    rows = c_rows.shape[0]
    tn = 1536
    return pl.pallas_call(
        _mod_kernel,
        grid=(depth, d6 // tn),
        in_specs=[
            pl.BlockSpec((rows, d), lambda i, n: (0, 0)),
            pl.BlockSpec((1, d, tn), lambda i, n: (i, 0, n)),
            pl.BlockSpec((1, 1, tn), lambda i, n: (i, 0, n)),
        ],
        out_specs=pl.BlockSpec((1, rows, tn), lambda i, n: (i, 0, n)),
        out_shape=jax.ShapeDtypeStruct((depth, rows, d6), F32),
        compiler_params=_params("arbitrary", "arbitrary"),
        name="adaln_modulation",
    )(c_rows, ada_w, ada_b.reshape(depth, 1, d6))


def _kv_store(ckv, k_rope, wuk_ref, wvt_ref, k_ref, v_ref):
    k_nope = _dot(ckv, wuk_ref[...])
    v_t = _dot_nt(wvt_ref[...], ckv)
    ones = jnp.ones((V_ROWS - V_HEAD, v_t.shape[1]), BF16)
    for hd in range(N_HEADS):
        k_ref[0, hd, :, 0:QK_NOPE] = k_nope[:, hd * QK_NOPE:(hd + 1) * QK_NOPE].astype(BF16)
        k_ref[0, hd, :, QK_NOPE:QK_DIM] = k_rope
        v_ref[0, hd, 0:V_HEAD, :] = v_t[hd * V_HEAD:(hd + 1) * V_HEAD, :].astype(BF16)
        v_ref[0, hd, V_HEAD:V_ROWS, :] = ones


def _qkv_latent_kernel(x_ref, mod_ref, nw_ref, qn_ref, kvn_ref, tc_ref, ts_ref, tct_ref, tst_ref,
                       wdn_ref, wuqt_ref, wuk_ref, wvt_ref, q_ref, k_ref, v_ref):
    shift, scale = mod_ref[0, 0:1, :], mod_ref[0, 1:2, :]
    h = (_rms(x_ref[0], nw_ref[...]) * (1.0 + scale) + shift).astype(BF16)
    down = _dot(h, wdn_ref[...])
    ckv = _rms(down[:, Q_LORA:Q_LORA + KV_LORA], kvn_ref[...]).astype(BF16)
    kr0 = Q_LORA + KV_LORA
    k_rope = (down[:, kr0:kr0 + QK_ROPE] * tc_ref[...]
              + down[:, kr0 + QK_ROPE:kr0 + 2 * QK_ROPE] * ts_ref[...]).astype(BF16)
    _kv_store(ckv, k_rope, wuk_ref, wvt_ref, k_ref, v_ref)

    cq = _rms(down[:, 0:Q_LORA], qn_ref[...]).astype(BF16)
    q_t = _dot_nt(wuqt_ref[...], cq) * (ATTN_SCALE * LOG2_E)
    r0 = N_HEADS * QK_NOPE
    r1 = r0 + N_HEADS * QK_ROPE
    for pair in range(N_HEADS // 2):
        lo = 2 * QK_ROPE * pair
        rope2 = (q_t[r0 + lo:r0 + lo + 2 * QK_ROPE, :] * tct_ref[...]
                 + q_t[r1 + lo:r1 + lo + 2 * QK_ROPE, :] * tst_ref[...]).astype(BF16)
        for sub in range(2):
            hd = 2 * pair + sub
            q_ref[0, hd, 0:QK_NOPE, :] = q_t[hd * QK_NOPE:(hd + 1) * QK_NOPE, :].astype(BF16)
            q_ref[0, hd, QK_NOPE:QK_DIM, :] = rope2[sub * QK_ROPE:(sub + 1) * QK_ROPE, :]


def _kv_context_kernel(x_ref, mod_ref, nw_ref, kvn_ref, wdn_ref, wuk_ref, wvt_ref, k_in, v_in, k_ref, v_ref):
    del k_in, v_in
    shift, scale = mod_ref[0:1, :], mod_ref[1:2, :]
    h = (_rms(x_ref[0], nw_ref[...]) * (1.0 + scale) + shift).astype(BF16)
    down = _dot(h, wdn_ref[...])
    ckv = _rms(down[:, Q_LORA:Q_LORA + KV_LORA], kvn_ref[...]).astype(BF16)
    kr0 = Q_LORA + KV_LORA
    k_rope = down[:, kr0:kr0 + QK_ROPE].astype(BF16)
    _kv_store(ckv, k_rope, wuk_ref, wvt_ref, k_ref, v_ref)


def _mla_qkv(x, ctx, mod, mod_ctx, norm_w, q_norm, kv_norm, tc, ts, tct, tst, w_down, w_uq_t, w_uk, w_v_t, tm):
    b, n, d = x.shape
    n_ctx = ctx.shape[1]
    n_all = n_ctx + n
    assert n % tm == 0 and n % n_ctx == 0
    k_shape = jax.ShapeDtypeStruct((b, N_HEADS, n_all, QK_DIM), BF16)
    v_shape = jax.ShapeDtypeStruct((b, N_HEADS, V_ROWS, n_all), BF16)
    q_t, k, v_t = pl.pallas_call(
        _qkv_latent_kernel,
        grid=(b, n // tm),
        in_specs=[
            pl.BlockSpec((1, tm, d), lambda bi, j: (bi, j, 0)),
            pl.BlockSpec((1, 2, d), lambda bi, j: (bi, 0, 0)),
            _resident((1, d)),
            _resident((1, Q_LORA)),
            _resident((1, KV_LORA)),
            pl.BlockSpec((tm, QK_ROPE), lambda bi, j: (j, 0)),
            pl.BlockSpec((tm, QK_ROPE), lambda bi, j: (j, 0)),
            pl.BlockSpec((2 * QK_ROPE, tm), lambda bi, j: (0, j)),
            pl.BlockSpec((2 * QK_ROPE, tm), lambda bi, j: (0, j)),
            _resident(w_down.shape),
            _resident(w_uq_t.shape),
            _resident(w_uk.shape),
            _resident(w_v_t.shape),
        ],
        out_specs=[
            pl.BlockSpec((1, N_HEADS, QK_DIM, tm), lambda bi, j: (bi, 0, 0, j)),
            pl.BlockSpec((1, N_HEADS, tm, QK_DIM), lambda bi, j: (bi, 0, j, 0)),
            pl.BlockSpec((1, N_HEADS, V_ROWS, tm), lambda bi, j: (bi, 0, 0, j)),
        ],
        out_shape=[jax.ShapeDtypeStruct((b, N_HEADS, QK_DIM, n), BF16), k_shape, v_shape],
        compiler_params=_params("arbitrary", "arbitrary"),
        name="mla_qkv_latent",
    )(x, mod, norm_w, q_norm, kv_norm, tc, ts, tct, tst, w_down, w_uq_t, w_uk, w_v_t)
    last = n // n_ctx
    k, v_t = pl.pallas_call(
        _kv_context_kernel,
        grid=(b,),
        in_specs=[
            pl.BlockSpec((1, n_ctx, d), lambda bi: (bi, 0, 0)),
            _resident((2, d)),
            _resident((1, d)),
            _resident((1, KV_LORA)),
            _resident(w_down.shape),
            _resident(w_uk.shape),
            _resident(w_v_t.shape),
            pl.BlockSpec(memory_space=pl.ANY),
            pl.BlockSpec(memory_space=pl.ANY),
        ],
        out_specs=[
            pl.BlockSpec((1, N_HEADS, n_ctx, QK_DIM), lambda bi: (bi, 0, last, 0)),
            pl.BlockSpec((1, N_HEADS, V_ROWS, n_ctx), lambda bi: (bi, 0, 0, last)),
        ],
        out_shape=[k_shape, v_shape],
        input_output_aliases={7: 0, 8: 1},
        compiler_params=_params("arbitrary"),
        name="mla_kv_context",
    )(ctx, mod_ctx, norm_w, kv_norm, w_down, w_uk, w_v_t, k, v_t)
    return q_t, k, v_t


def _attn_kernel(q_ref, k_ref, v_ref, o_ref, *, tk):
    q_t = q_ref[0, 0]
    tq = q_t.shape[1]
    n_keys = k_ref.shape[2]
    n_chunks = n_keys // tk

    def scores(c):
        return _dot(k_ref[0, 0, c * tk:(c + 1) * tk, :], q_t)

    def pv(c, p):
        return _dot(v_ref[0, 0, :, c * tk:(c + 1) * tk], p)

    def finish(acc):
        o_t = acc[0:V_HEAD, :] / acc[V_HEAD:V_HEAD + 1, :]
        o_ref[0] = o_t.T.astype(BF16)

    s = scores(0)
    ref = jnp.max(s, axis=0, keepdims=True)
    p = jnp.exp2(s - ref).astype(BF16)
    s_nxt = scores(1) if n_chunks > 1 else None
    acc = pv(0, p)
    acc_ref = ref
    over = jnp.zeros((1, tq), F32)
    for c in range(1, n_chunks):
        s = s_nxt
        if c + 1 < n_chunks:
            s_nxt = scores(c + 1)
        chunk_max = jnp.max(s, axis=0, keepdims=True)
        p = jnp.exp2(s - ref).astype(BF16)
        acc = jnp.exp2(acc_ref - ref) * acc + pv(c, p)
        acc_ref = ref
        over = jnp.maximum(over, chunk_max - ref)
        ref = jnp.maximum(ref, chunk_max)
    finish(acc)

    @pl.when(jnp.max(over) > LAG_LIMIT)
    def _():
        def body(c, carry):
            m, acc = carry
            k0 = pl.multiple_of(c * tk, tk)
            s = _dot(k_ref[0, 0, pl.ds(k0, tk), :], q_t)
            m_new = jnp.maximum(m, jnp.max(s, axis=0, keepdims=True))
            p = jnp.exp2(s - m_new).astype(BF16)
            acc = jnp.exp2(m - m_new) * acc + _dot(v_ref[0, 0, :, pl.ds(k0, tk)], p)
            return m_new, acc

        init = (jnp.full((1, tq), -jnp.inf, F32), jnp.zeros((V_ROWS, tq), F32))
        finish(lax.fori_loop(0, n_chunks, body, init)[1])


def _attention(q_t, k, v_t, tq, tk):
    b, nh, _, n = q_t.shape
    n_all = k.shape[2]
    assert n % tq == 0 and n_all % tk == 0
    return pl.pallas_call(
        functools.partial(_attn_kernel, tk=tk),
        grid=(b, nh, n // tq),
        in_specs=[
            pl.BlockSpec((1, 1, QK_DIM, tq), lambda bi, hi, i: (bi, hi, 0, i)),
            pl.BlockSpec((1, 1, n_all, QK_DIM), lambda bi, hi, i: (bi, hi, 0, 0)),
            pl.BlockSpec((1, 1, V_ROWS, n_all), lambda bi, hi, i: (bi, hi, 0, 0)),
        ],
        out_specs=pl.BlockSpec((1, tq, V_HEAD), lambda bi, hi, i: (bi, i, hi)),
        out_shape=jax.ShapeDtypeStruct((b, n, nh * V_HEAD), BF16),
        compiler_params=_params("arbitrary", "arbitrary", "arbitrary"),
        name="mla_attention",
    )(q_t, k, v_t)


def _ffn_kernel(*refs, with_proj, with_final):
    refs = list(refs)
    x_ref = refs.pop(0)
    o_ref = refs.pop(0) if with_proj else None
    mod_ref = refs.pop(0)
    nw_ref = refs.pop(0)
    wo_ref = refs.pop(0) if with_proj else None
    wg_ref, wu_ref, wd_ref = refs.pop(0), refs.pop(0), refs.pop(0)
    fw_ref = refs.pop(0) if with_final else None
    out_ref = refs.pop(0)

    x = x_ref[0]
    if with_proj:
        x = x + mod_ref[0, 0:1, :] * _dot(o_ref[0], wo_ref[...])
    shift, scale, gate = mod_ref[0, 1:2, :], mod_ref[0, 2:3, :], mod_ref[0, 3:4, :]
    h = (_rms(x, nw_ref[...]) * (1.0 + scale) + shift).astype(BF16)
    g = _dot(h, wg_ref[...])
    u = _dot(h, wu_ref[...])
    act = (g * jax.nn.sigmoid(g) * u).astype(BF16)
    x = x + gate * _dot(act, wd_ref[...])
    if with_final:
        x = _rms(x, fw_ref[...])
    out_ref[0] = x


def _ffn(x, o, mod, norm_w, w_o, w_gate, w_up, w_down, final_w, tm):
    b, n, d = x.shape
    with_proj = o is not None
    with_final = final_w is not None
    tile = pl.BlockSpec((1, tm, d), lambda bi, i: (bi, i, 0))
    args, specs = [x], [tile]
    if with_proj:
        args.append(o)
        specs.append(tile)
    args += [mod, norm_w]
    specs += [pl.BlockSpec((1, 4, d), lambda bi, i: (bi, 0, 0)), _resident((1, d))]
    if with_proj:
        args.append(w_o)
        specs.append(_resident(w_o.shape))
    args += [w_gate, w_up, w_down]
    specs += [_resident(w_gate.shape), _resident(w_up.shape), _resident(w_down.shape)]
    if with_final:
        args.append(final_w)
        specs.append(_resident((1, d)))
    return pl.pallas_call(
        functools.partial(_ffn_kernel, with_proj=with_proj, with_final=with_final),
        grid=(b, n // tm),
        in_specs=specs,
        out_specs=tile,
        out_shape=jax.ShapeDtypeStruct((b, n, d), F32),
        compiler_params=_params("arbitrary", "arbitrary"),
        name="attn_proj_ffn" if with_proj else "ffn_final",
    )(*args)


def _conv_kernel(x_ref, xp_ref, xn_ref, mod_ref, nw_ref, win_ref, cw_ref, wout_ref, out_ref, z_ref):
    i = pl.program_id(1)
    last = pl.num_programs(1) - 1
    d = x_ref.shape[2]
    tm = x_ref.shape[1]
    x = x_ref[0]
    xe = jnp.concatenate([xp_ref[0], x, xn_ref[0]], axis=0)
    shift, scale, gate = mod_ref[0, 0:1, :], mod_ref[0, 1:2, :], mod_ref[0, 2:3, :]
    h = (_rms(xe, nw_ref[...]) * (1.0 + scale) + shift).astype(BF16)
    p = _dot(h, win_ref[...])
    z = p[:, d:2 * d] * p[:, 2 * d:3 * d]
    row = lax.broadcasted_iota(jnp.int32, (tm + 2 * HALO, 1), 0)
    outside = ((row < HALO) & (i == 0)) | ((row >= tm + HALO) & (i == last))
    z_ref[...] = jnp.where(outside, 0.0, z)
    u = (cw_ref[0:1, :] * z_ref[HALO - 1:HALO - 1 + tm, :]
         + cw_ref[1:2, :] * z_ref[HALO:HALO + tm, :]
         + cw_ref[2:3, :] * z_ref[HALO + 1:HALO + 1 + tm, :])
    y = _dot((p[HALO:HALO + tm, 0:d] * u).astype(BF16), wout_ref[...])
    out_ref[0] = x + gate * y


def _short_conv(x, mod, norm_w, w_in, conv_w, w_out, tm):
    b, n, d = x.shape
    nblk = n // HALO
    per = tm // HALO
    return pl.pallas_call(
        _conv_kernel,
        grid=(b, n // tm),
        in_specs=[
            pl.BlockSpec((1, tm, d), lambda bi, i: (bi, i, 0)),
            pl.BlockSpec((1, HALO, d), lambda bi, i: (bi, jnp.maximum(i * per - 1, 0), 0)),
            pl.BlockSpec((1, HALO, d), lambda bi, i: (bi, jnp.minimum((i + 1) * per, nblk - 1), 0)),
            pl.BlockSpec((1, 3, d), lambda bi, i: (bi, 0, 0)),
            _resident((1, d)),
            _resident(w_in.shape),
            _resident(conv_w.shape),
            _resident(w_out.shape),
        ],
        out_specs=pl.BlockSpec((1, tm, d), lambda bi, i: (bi, i, 0)),
        out_shape=jax.ShapeDtypeStruct((b, n, d), F32),
        scratch_shapes=[pltpu.VMEM((tm + 2 * HALO, d), F32)],
        compiler_params=_params("arbitrary", "arbitrary"),
        name="short_conv",
    )(x, x, x, mod, norm_w, w_in, conv_w, w_out)


def _rope_tables(n):
    rows = n // GRID_W
    n_freq = QK_ROPE // 4
    freqs = ROPE_THETA ** (-jnp.arange(n_freq, dtype=F32) / n_freq)
    ar = jnp.arange(rows).astype(F32)[:, None] * freqs
    ac = jnp.arange(GRID_W).astype(F32)[:, None] * freqs

    def table(fn):
        r = jnp.broadcast_to(fn(ar)[:, None, :], (rows, GRID_W, n_freq)).reshape(n, n_freq)
        c = jnp.broadcast_to(fn(ac)[None, :, :], (rows, GRID_W, n_freq)).reshape(n, n_freq)
        return jnp.concatenate([r, r, c, c], axis=-1)

    return table(jnp.cos), table(jnp.sin)


def _rotate_cols(w):
    q = QK_ROPE // 4
    perm = np.concatenate([np.arange(q, 2 * q), np.arange(0, q), np.arange(3 * q, 4 * q), np.arange(2 * q, 3 * q)])
    sign = np.concatenate([-np.ones(q), np.ones(q), -np.ones(q), np.ones(q)]).astype(np.float32)
    return w[..., perm] * sign


def kernel(x, c, ctx, c_ctx, ada_w, ada_b, norm_mix, norm_ffn, mla_w_dq, mla_q_norm, mla_w_uq, mla_w_dkv,
           mla_kv_norm, mla_w_ukv, mla_w_o, conv_w_in, conv_w, conv_w_out, ffn_w_gate, ffn_w_up, ffn_w_down,
           final_norm):
    b, n, d = x.shape
    n_ctx = ctx.shape[1]

    pad = (-(b + 1)) % 8
    c_rows = jnp.concatenate([c, c_ctx[None, :], jnp.zeros((pad, d), F32)], axis=0)
    mods = _modulation(c_rows, ada_w, ada_b)
    m0 = mods[0, :b].reshape(b, 6, d)
    m1 = mods[1, :b].reshape(b, 6, d)
    mod_ctx = mods[0, b].reshape(6, d)[0:2]

    w_uq = mla_w_uq[0].reshape(Q_LORA, N_HEADS, QK_DIM)
    uq_rope = w_uq[:, :, QK_NOPE:]
    w_uq_t = jnp.concatenate([w_uq[:, :, :QK_NOPE].reshape(Q_LORA, -1), uq_rope.reshape(Q_LORA, -1),
                              _rotate_cols(uq_rope).reshape(Q_LORA, -1)], axis=1).T.astype(BF16)
    w_dkv = mla_w_dkv[0]
    w_down = jnp.concatenate([mla_w_dq[0], w_dkv, _rotate_cols(w_dkv[:, KV_LORA:])], axis=1).astype(BF16)
    w_ukv = mla_w_ukv[0].reshape(KV_LORA, N_HEADS, QK_NOPE + V_HEAD)
    w_uk = w_ukv[:, :, :QK_NOPE].reshape(KV_LORA, -1).astype(BF16)
    w_v_t = w_ukv[:, :, QK_NOPE:].reshape(KV_LORA, -1).T.astype(BF16)
    tc, ts = _rope_tables(n)
    tct = jnp.tile(tc.T, (2, 1))
    tst = jnp.tile(ts.T, (2, 1))

    q_t, k, v_t = _mla_qkv(x, ctx, m0[:, 0:2], mod_ctx, norm_mix[0][None], mla_q_norm[0][None],
                           mla_kv_norm[0][None], tc, ts, tct, tst, w_down, w_uq_t, w_uk, w_v_t, tm=min(512, n))
    n_all = n + n_ctx
    tk = 768 if n_all % 768 == 0 else n_ctx
    o = _attention(q_t, k, v_t, tq=min(1024, n), tk=tk)
    x = _ffn(x, o, m0[:, 2:6], norm_ffn[0][None], mla_w_o[0].astype(BF16), ffn_w_gate[0].astype(BF16),
             ffn_w_up[0].astype(BF16), ffn_w_down[0].astype(BF16), None, tm=min(512, n))

    x = _short_conv(x, m1[:, 0:3], norm_mix[1][None], conv_w_in[0].astype(BF16), conv_w[0],
                    conv_w_out[0].astype(BF16), tm=min(512, n))
    return _ffn(x, None, m1[:, 2:6], norm_ffn[1][None], None, ffn_w_gate[1].astype(BF16), ffn_w_up[1].astype(BF16),
                ffn_w_down[1].astype(BF16), final_norm[None], tm=min(512, n))
```

```python
import functools
import math

import numpy as np
import jax
import jax.numpy as jnp
from jax import lax
from jax.experimental import pallas as pl
from jax.experimental.pallas import tpu as pltpu

D_MODEL = 1024
GRID_W = 64
N_HEADS = 8
QK_NOPE = 128
QK_ROPE = 64
V_HEAD = 128
Q_LORA = 256
KV_LORA = 128
QK_DIM = QK_NOPE + QK_ROPE
V_ROWS = V_HEAD + 16
ROPE_THETA = 10000.0
NORM_EPS = 1e-6
ATTN_SCALE = 1.0 / math.sqrt(QK_NOPE + QK_ROPE)
LOG2_E = 1.4426950408889634
LAG_LIMIT = 60.0

F32 = jnp.float32
BF16 = jnp.bfloat16

VMEM_LIMIT_BYTES = 56 * 1024 * 1024
HALO = 8


def _params(*sem):
    return pltpu.CompilerParams(dimension_semantics=sem, vmem_limit_bytes=VMEM_LIMIT_BYTES)


def _resident(shape):
    zeros = (0,) * len(shape)
    return pl.BlockSpec(shape, lambda *_: zeros, pipeline_mode=pl.Buffered(1))


def _rms(x, w):
    return x * lax.rsqrt(jnp.mean(x * x, axis=-1, keepdims=True) + NORM_EPS) * w


def _dot(a, b):
    return jnp.dot(a, b, preferred_element_type=F32)


def _dot_nt(a, b):
    return lax.dot_general(a, b, (((1,), (1,)), ((), ())), preferred_element_type=F32)


def _mod_kernel(c_ref, w_ref, b_ref, o_ref):
    c = c_ref[...]
    o_ref[0] = _dot(c * jax.nn.sigmoid(c), w_ref[0]) + b_ref[0]


def _modulation(c_rows, ada_w, ada_b):
    depth, d, d6 = ada_w.shape
    rows = c_rows.shape[0]
    tn = 1536
    return pl.pallas_call(
        _mod_kernel,
        grid=(depth, d6 // tn),
        in_specs=[
            pl.BlockSpec((rows, d), lambda i, n: (0, 0)),
            pl.BlockSpec((1, d, tn), lambda i, n: (i, 0, n)),
            pl.BlockSpec((1, 1, tn), lambda i, n: (i, 0, n)),
        ],
        out_specs=pl.BlockSpec((1, rows, tn), lambda i, n: (i, 0, n)),
        out_shape=jax.ShapeDtypeStruct((depth, rows, d6), F32),
        compiler_params=_params("arbitrary", "arbitrary"),
        name="adaln_modulation",
    )(c_rows, ada_w, ada_b.reshape(depth, 1, d6))


def _kv_store(ckv, k_rope, wuk_ref, wvt_ref, k_ref, v_ref):
    k_nope = _dot(ckv, wuk_ref[...])
    v_t = _dot_nt(wvt_ref[...], ckv)
    ones = jnp.ones((V_ROWS - V_HEAD, v_t.shape[1]), BF16)
    for hd in range(N_HEADS):
        k_ref[0, hd, :, 0:QK_NOPE] = k_nope[:, hd * QK_NOPE:(hd + 1) * QK_NOPE].astype(BF16)
        k_ref[0, hd, :, QK_NOPE:QK_DIM] = k_rope
        v_ref[0, hd, 0:V_HEAD, :] = v_t[hd * V_HEAD:(hd + 1) * V_HEAD, :].astype(BF16)
        v_ref[0, hd, V_HEAD:V_ROWS, :] = ones


def _qkv_latent_kernel(x_ref, mod_ref, nw_ref, qn_ref, kvn_ref, tc_ref, ts_ref, tct_ref, tst_ref,
                       wdn_ref, wuqt_ref, wuk_ref, wvt_ref, q_ref, k_ref, v_ref):
    shift, scale = mod_ref[0, 0:1, :], mod_ref[0, 1:2, :]
    h = (_rms(x_ref[0], nw_ref[...]) * (1.0 + scale) + shift).astype(BF16)
    down = _dot(h, wdn_ref[...])
    ckv = _rms(down[:, Q_LORA:Q_LORA + KV_LORA], kvn_ref[...]).astype(BF16)
    kr0 = Q_LORA + KV_LORA
    k_rope = (down[:, kr0:kr0 + QK_ROPE] * tc_ref[...]
              + down[:, kr0 + QK_ROPE:kr0 + 2 * QK_ROPE] * ts_ref[...]).astype(BF16)
    _kv_store(ckv, k_rope, wuk_ref, wvt_ref, k_ref, v_ref)

    cq = _rms(down[:, 0:Q_LORA], qn_ref[...]).astype(BF16)
    q_t = _dot_nt(wuqt_ref[...], cq) * (ATTN_SCALE * LOG2_E)
    r0 = N_HEADS * QK_NOPE
    r1 = r0 + N_HEADS * QK_ROPE
    for pair in range(N_HEADS // 2):
        lo = 2 * QK_ROPE * pair
        rope2 = (q_t[r0 + lo:r0 + lo + 2 * QK_ROPE, :] * tct_ref[...]
                 + q_t[r1 + lo:r1 + lo + 2 * QK_ROPE, :] * tst_ref[...]).astype(BF16)
        for sub in range(2):
            hd = 2 * pair + sub
            q_ref[0, hd, 0:QK_NOPE, :] = q_t[hd * QK_NOPE:(hd + 1) * QK_NOPE, :].astype(BF16)
            q_ref[0, hd, QK_NOPE:QK_DIM, :] = rope2[sub * QK_ROPE:(sub + 1) * QK_ROPE, :]


def _kv_context_kernel(x_ref, mod_ref, nw_ref, kvn_ref, wdn_ref, wuk_ref, wvt_ref, k_ref, v_ref):
    shift, scale = mod_ref[0:1, :], mod_ref[1:2, :]
    h = (_rms(x_ref[0], nw_ref[...]) * (1.0 + scale) + shift).astype(BF16)
    down = _dot(h, wdn_ref[...])
    ckv = _rms(down[:, Q_LORA:Q_LORA + KV_LORA], kvn_ref[...]).astype(BF16)
    kr0 = Q_LORA + KV_LORA
    k_rope = down[:, kr0:kr0 + QK_ROPE].astype(BF16)
    _kv_store(ckv, k_rope, wuk_ref, wvt_ref, k_ref, v_ref)


def _mla_qkv(x, ctx, mod, mod_ctx, norm_w, q_norm, kv_norm, tc, ts, tct, tst, w_down, w_uq_t, w_uk, w_v_t, tm):
    b, n, d = x.shape
    n_ctx = ctx.shape[1]
    assert n % tm == 0
    q_t, k, v_t = pl.pallas_call(
        _qkv_latent_kernel,
        grid=(b, n // tm),
        in_specs=[
            pl.BlockSpec((1, tm, d), lambda bi, j: (bi, j, 0)),
            pl.BlockSpec((1, 2, d), lambda bi, j: (bi, 0, 0)),
            _resident((1, d)),
            _resident((1, Q_LORA)),
            _resident((1, KV_LORA)),
            pl.BlockSpec((tm, QK_ROPE), lambda bi, j: (j, 0)),
            pl.BlockSpec((tm, QK_ROPE), lambda bi, j: (j, 0)),
            pl.BlockSpec((2 * QK_ROPE, tm), lambda bi, j: (0, j)),
            pl.BlockSpec((2 * QK_ROPE, tm), lambda bi, j: (0, j)),
            _resident(w_down.shape),
            _resident(w_uq_t.shape),
            _resident(w_uk.shape),
            _resident(w_v_t.shape),
        ],
        out_specs=[
            pl.BlockSpec((1, N_HEADS, QK_DIM, tm), lambda bi, j: (bi, 0, 0, j)),
            pl.BlockSpec((1, N_HEADS, tm, QK_DIM), lambda bi, j: (bi, 0, j, 0)),
            pl.BlockSpec((1, N_HEADS, V_ROWS, tm), lambda bi, j: (bi, 0, 0, j)),
        ],
        out_shape=[jax.ShapeDtypeStruct((b, N_HEADS, QK_DIM, n), BF16),
                   jax.ShapeDtypeStruct((b, N_HEADS, n, QK_DIM), BF16),
                   jax.ShapeDtypeStruct((b, N_HEADS, V_ROWS, n), BF16)],
        compiler_params=_params("arbitrary", "arbitrary"),
        name="mla_qkv_latent",
    )(x, mod, norm_w, q_norm, kv_norm, tc, ts, tct, tst, w_down, w_uq_t, w_uk, w_v_t)
    k_ctx, v_ctx_t = pl.pallas_call(
        _kv_context_kernel,
        grid=(b,),
        in_specs=[
            pl.BlockSpec((1, n_ctx, d), lambda bi: (bi, 0, 0)),
            _resident((2, d)),
            _resident((1, d)),
            _resident((1, KV_LORA)),
            _resident(w_down.shape),
            _resident(w_uk.shape),
            _resident(w_v_t.shape),
        ],
        out_specs=[
            pl.BlockSpec((1, N_HEADS, n_ctx, QK_DIM), lambda bi: (bi, 0, 0, 0)),
            pl.BlockSpec((1, N_HEADS, V_ROWS, n_ctx), lambda bi: (bi, 0, 0, 0)),
        ],
        out_shape=[jax.ShapeDtypeStruct((b, N_HEADS, n_ctx, QK_DIM), BF16),
                   jax.ShapeDtypeStruct((b, N_HEADS, V_ROWS, n_ctx), BF16)],
        compiler_params=_params("arbitrary"),
        name="mla_kv_context",
    )(ctx, mod_ctx, norm_w, kv_norm, w_down, w_uk, w_v_t)
    return q_t, k, v_t, k_ctx, v_ctx_t


def _attn_kernel(q_ref, k_ref, v_ref, kc_ref, vc_ref, o_ref, *, tk):
    q_t = q_ref[0, 0]
    tq = q_t.shape[1]
    n_chunks = k_ref.shape[2] // tk

    def scores(c):
        return _dot(k_ref[0, 0, c * tk:(c + 1) * tk, :], q_t)

    def pv(c, p):
        return _dot(v_ref[0, 0, :, c * tk:(c + 1) * tk], p)

    def finish(acc):
        o_t = acc[0:V_HEAD, :] / acc[V_HEAD:V_HEAD + 1, :]
        o_ref[0] = o_t.T.astype(BF16)

    def context_softmax():
        s = _dot(kc_ref[0, 0], q_t)
        ref = jnp.max(s, axis=0, keepdims=True)
        return ref, jnp.exp2(s - ref).astype(BF16)

    ref, p_ctx = context_softmax()
    ahead = [scores(c) for c in range(min(2, n_chunks))]
    acc = _dot(vc_ref[0, 0], p_ctx)
    acc_ref = ref
    over = jnp.zeros((1, tq), F32)
    for c in range(n_chunks):
        s = ahead.pop(0)
        if c + 2 < n_chunks:
            ahead.append(scores(c + 2))
        chunk_max = jnp.max(s, axis=0, keepdims=True)
        p = jnp.exp2(s - ref).astype(BF16)
        acc = jnp.exp2(acc_ref - ref) * acc + pv(c, p)
        acc_ref = ref
        over = jnp.maximum(over, chunk_max - ref)
        ref = jnp.maximum(ref, chunk_max)
    finish(acc)

    @pl.when(jnp.max(over) > LAG_LIMIT)
    def _():
        def body(c, carry):
            m, acc = carry
            k0 = pl.multiple_of(c * tk, tk)
            s = _dot(k_ref[0, 0, pl.ds(k0, tk), :], q_t)
            m_new = jnp.maximum(m, jnp.max(s, axis=0, keepdims=True))
            p = jnp.exp2(s - m_new).astype(BF16)
            acc = jnp.exp2(m - m_new) * acc + _dot(v_ref[0, 0, :, pl.ds(k0, tk)], p)
            return m_new, acc

        m0, p0 = context_softmax()
        finish(lax.fori_loop(0, n_chunks, body, (m0, _dot(vc_ref[0, 0], p0)))[1])


def _attention(q_t, k, v_t, k_ctx, v_ctx_t, tq, tk):
    b, nh, _, n = q_t.shape
    n_ctx = k_ctx.shape[2]
    assert n % tq == 0 and n % tk == 0
    head = lambda bi, hi, i: (bi, hi, 0, 0)
    return pl.pallas_call(
        functools.partial(_attn_kernel, tk=tk),
        grid=(b, nh, n // tq),
        in_specs=[
            pl.BlockSpec((1, 1, QK_DIM, tq), lambda bi, hi, i: (bi, hi, 0, i)),
            pl.BlockSpec((1, 1, n, QK_DIM), head),
            pl.BlockSpec((1, 1, V_ROWS, n), head),
            pl.BlockSpec((1, 1, n_ctx, QK_DIM), head),
            pl.BlockSpec((1, 1, V_ROWS, n_ctx), head),
        ],
        out_specs=pl.BlockSpec((1, tq, V_HEAD), lambda bi, hi, i: (bi, i, hi)),
        out_shape=jax.ShapeDtypeStruct((b, n, nh * V_HEAD), BF16),
        compiler_params=_params("arbitrary", "arbitrary", "arbitrary"),
        name="mla_attention",
    )(q_t, k, v_t, k_ctx, v_ctx_t)


def _ffn_kernel(*refs, with_proj, with_final):
    refs = list(refs)
    x_ref = refs.pop(0)
    o_ref = refs.pop(0) if with_proj else None
    mod_ref = refs.pop(0)
    nw_ref = refs.pop(0)
    wo_ref = refs.pop(0) if with_proj else None
    wg_ref, wu_ref, wd_ref = refs.pop(0), refs.pop(0), refs.pop(0)
    fw_ref = refs.pop(0) if with_final else None
    out_ref = refs.pop(0)

    x = x_ref[0]
    if with_proj:
        x = x + mod_ref[0, 0:1, :] * _dot(o_ref[0], wo_ref[...])
    shift, scale, gate = mod_ref[0, 1:2, :], mod_ref[0, 2:3, :], mod_ref[0, 3:4, :]
    h = (_rms(x, nw_ref[...]) * (1.0 + scale) + shift).astype(BF16)
    g = _dot(h, wg_ref[...])
    u = _dot(h, wu_ref[...])
    act = (g * jax.nn.sigmoid(g) * u).astype(BF16)
    x = x + gate * _dot(act, wd_ref[...])
    if with_final:
        x = _rms(x, fw_ref[...])
    out_ref[0] = x


def _ffn(x, o, mod, norm_w, w_o, w_gate, w_up, w_down, final_w, tm):
    b, n, d = x.shape
    with_proj = o is not None
    with_final = final_w is not None
    tile = pl.BlockSpec((1, tm, d), lambda bi, i: (bi, i, 0))
    args, specs = [x], [tile]
    if with_proj:
        args.append(o)
        specs.append(tile)
    args += [mod, norm_w]
    specs += [pl.BlockSpec((1, 4, d), lambda bi, i: (bi, 0, 0)), _resident((1, d))]
    if with_proj:
        args.append(w_o)
        specs.append(_resident(w_o.shape))
    args += [w_gate, w_up, w_down]
    specs += [_resident(w_gate.shape), _resident(w_up.shape), _resident(w_down.shape)]
    if with_final:
        args.append(final_w)
        specs.append(_resident((1, d)))
    return pl.pallas_call(
        functools.partial(_ffn_kernel, with_proj=with_proj, with_final=with_final),
        grid=(b, n // tm),
        in_specs=specs,
        out_specs=tile,
        out_shape=jax.ShapeDtypeStruct((b, n, d), F32),
        compiler_params=_params("arbitrary", "arbitrary"),
        name="attn_proj_ffn" if with_proj else "ffn_final",
    )(*args)


def _conv_kernel(x_ref, xp_ref, xn_ref, mod_ref, nw_ref, win_ref, cw_ref, wout_ref, out_ref, z_ref):
    i = pl.program_id(1)
    last = pl.num_programs(1) - 1
    d = x_ref.shape[2]
    tm = x_ref.shape[1]
    x = x_ref[0]
    xe = jnp.concatenate([xp_ref[0], x, xn_ref[0]], axis=0)
    shift, scale, gate = mod_ref[0, 0:1, :], mod_ref[0, 1:2, :], mod_ref[0, 2:3, :]
    h = (_rms(xe, nw_ref[...]) * (1.0 + scale) + shift).astype(BF16)
    p = _dot(h, win_ref[...])
    z = p[:, d:2 * d] * p[:, 2 * d:3 * d]
    row = lax.broadcasted_iota(jnp.int32, (tm + 2 * HALO, 1), 0)
    outside = ((row < HALO) & (i == 0)) | ((row >= tm + HALO) & (i == last))
    z_ref[...] = jnp.where(outside, 0.0, z)
    u = (cw_ref[0:1, :] * z_ref[HALO - 1:HALO - 1 + tm, :]
         + cw_ref[1:2, :] * z_ref[HALO:HALO + tm, :]
         + cw_ref[2:3, :] * z_ref[HALO + 1:HALO + 1 + tm, :])
    y = _dot((p[HALO:HALO + tm, 0:d] * u).astype(BF16), wout_ref[...])
    out_ref[0] = x + gate * y


def _short_conv(x, mod, norm_w, w_in, conv_w, w_out, tm):
    b, n, d = x.shape
    nblk = n // HALO
    per = tm // HALO
    return pl.pallas_call(
        _conv_kernel,
        grid=(b, n // tm),
        in_specs=[
            pl.BlockSpec((1, tm, d), lambda bi, i: (bi, i, 0)),
            pl.BlockSpec((1, HALO, d), lambda bi, i: (bi, jnp.maximum(i * per - 1, 0), 0)),
            pl.BlockSpec((1, HALO, d), lambda bi, i: (bi, jnp.minimum((i + 1) * per, nblk - 1), 0)),
            pl.BlockSpec((1, 3, d), lambda bi, i: (bi, 0, 0)),
            _resident((1, d)),
            _resident(w_in.shape),
            _resident(conv_w.shape),
            _resident(w_out.shape),
        ],
        out_specs=pl.BlockSpec((1, tm, d), lambda bi, i: (bi, i, 0)),
        out_shape=jax.ShapeDtypeStruct((b, n, d), F32),
        scratch_shapes=[pltpu.VMEM((tm + 2 * HALO, d), F32)],
        compiler_params=_params("arbitrary", "arbitrary"),
        name="short_conv",
    )(x, x, x, mod, norm_w, w_in, conv_w, w_out)


def _rope_tables(n):
    rows = n // GRID_W
    n_freq = QK_ROPE // 4
    freqs = ROPE_THETA ** (-jnp.arange(n_freq, dtype=F32) / n_freq)
    ar = jnp.arange(rows).astype(F32)[:, None] * freqs
    ac = jnp.arange(GRID_W).astype(F32)[:, None] * freqs

    def table(fn):
        r = jnp.broadcast_to(fn(ar)[:, None, :], (rows, GRID_W, n_freq)).reshape(n, n_freq)
        c = jnp.broadcast_to(fn(ac)[None, :, :], (rows, GRID_W, n_freq)).reshape(n, n_freq)
        return jnp.concatenate([r, r, c, c], axis=-1)

    return table(jnp.cos), table(jnp.sin)


def _rotate_cols(w):
    q = QK_ROPE // 4
    perm = np.concatenate([np.arange(q, 2 * q), np.arange(0, q), np.arange(3 * q, 4 * q), np.arange(2 * q, 3 * q)])
    sign = np.concatenate([-np.ones(q), np.ones(q), -np.ones(q), np.ones(q)]).astype(np.float32)
    return w[..., perm] * sign


def kernel(x, c, ctx, c_ctx, ada_w, ada_b, norm_mix, norm_ffn, mla_w_dq, mla_q_norm, mla_w_uq, mla_w_dkv,
           mla_kv_norm, mla_w_ukv, mla_w_o, conv_w_in, conv_w, conv_w_out, ffn_w_gate, ffn_w_up, ffn_w_down,
           final_norm):
    b, n, d = x.shape
    n_ctx = ctx.shape[1]

    pad = (-(b + 1)) % 8
    c_rows = jnp.concatenate([c, c_ctx[None, :], jnp.zeros((pad, d), F32)], axis=0)
    mods = _modulation(c_rows, ada_w, ada_b)
    m0 = mods[0, :b].reshape(b, 6, d)
    m1 = mods[1, :b].reshape(b, 6, d)
    mod_ctx = mods[0, b].reshape(6, d)[0:2]

    w_uq = mla_w_uq[0].reshape(Q_LORA, N_HEADS, QK_DIM)
    uq_rope = w_uq[:, :, QK_NOPE:]
    w_uq_t = jnp.concatenate([w_uq[:, :, :QK_NOPE].reshape(Q_LORA, -1), uq_rope.reshape(Q_LORA, -1),
                              _rotate_cols(uq_rope).reshape(Q_LORA, -1)], axis=1).T.astype(BF16)
    w_dkv = mla_w_dkv[0]
    w_down = jnp.concatenate([mla_w_dq[0], w_dkv, _rotate_cols(w_dkv[:, KV_LORA:])], axis=1).astype(BF16)
    w_ukv = mla_w_ukv[0].reshape(KV_LORA, N_HEADS, QK_NOPE + V_HEAD)
    w_uk = w_ukv[:, :, :QK_NOPE].reshape(KV_LORA, -1).astype(BF16)
    w_v_t = w_ukv[:, :, QK_NOPE:].reshape(KV_LORA, -1).T.astype(BF16)
    tc, ts = _rope_tables(n)
    tct = jnp.tile(tc.T, (2, 1))
    tst = jnp.tile(ts.T, (2, 1))

    q_t, k, v_t, k_ctx, v_ctx_t = _mla_qkv(x, ctx, m0[:, 0:2], mod_ctx, norm_mix[0][None], mla_q_norm[0][None],
                                           mla_kv_norm[0][None], tc, ts, tct, tst, w_down, w_uq_t, w_uk, w_v_t,
                                           tm=min(512, n))
    o = _attention(q_t, k, v_t, k_ctx, v_ctx_t, tq=min(1024, n), tk=min(512, n))
    x = _ffn(x, o, m0[:, 2:6], norm_ffn[0][None], mla_w_o[0].astype(BF16), ffn_w_gate[0].astype(BF16),
             ffn_w_up[0].astype(BF16), ffn_w_down[0].astype(BF16), None, tm=min(512, n))

    x = _short_conv(x, m1[:, 0:3], norm_mix[1][None], conv_w_in[0].astype(BF16), conv_w[0],
                    conv_w_out[0].astype(BF16), tm=min(512, n))
    return _ffn(x, None, m1[:, 2:6], norm_ffn[1][None], None, ffn_w_gate[1].astype(BF16), ffn_w_up[1].astype(BF16),
                ffn_w_down[1].astype(BF16), final_norm[None], tm=min(512, n))
```

```python
import functools
import math

import numpy as np
import jax
import jax.numpy as jnp
from jax import lax
from jax.experimental import pallas as pl
from jax.experimental.pallas import tpu as pltpu

D_MODEL = 1024
GRID_W = 64
N_HEADS = 8
QK_NOPE = 128
QK_ROPE = 64
V_HEAD = 128
Q_LORA = 256
KV_LORA = 128
QK_DIM = QK_NOPE + QK_ROPE
V_ROWS = V_HEAD + 16
ROPE_THETA = 10000.0
NORM_EPS = 1e-6
ATTN_SCALE = 1.0 / math.sqrt(QK_NOPE + QK_ROPE)
LOG2_E = 1.4426950408889634
LAG_LIMIT = 60.0
ATTN_TQ = 1024
ATTN_TK = 512

F32 = jnp.float32
BF16 = jnp.bfloat16

VMEM_LIMIT_BYTES = 56 * 1024 * 1024
HALO = 8


def _params(*sem):
    return pltpu.CompilerParams(dimension_semantics=sem, vmem_limit_bytes=VMEM_LIMIT_BYTES)


def _resident(shape):
    zeros = (0,) * len(shape)
    return pl.BlockSpec(shape, lambda *_: zeros, pipeline_mode=pl.Buffered(1))


def _rms(x, w):
    return x * lax.rsqrt(jnp.mean(x * x, axis=-1, keepdims=True) + NORM_EPS) * w


def _rms_mod(x, w, shift, scale):
    return x * lax.rsqrt(jnp.mean(x * x, axis=-1, keepdims=True) + NORM_EPS) * (w * (1.0 + scale)) + shift


def _dot(a, b):
    return jnp.dot(a, b, preferred_element_type=F32)


def _dot_nt(a, b):
    return lax.dot_general(a, b, (((1,), (1,)), ((), ())), preferred_element_type=F32)


def _mod_kernel(c_ref, w_ref, b_ref, o_ref):
    c = c_ref[...]
    o_ref[0] = _dot(c * jax.nn.sigmoid(c), w_ref[0]) + b_ref[0]


def _modulation(c_rows, ada_w, ada_b):
    depth, d, d6 = ada_w.shape
    rows = c_rows.shape[0]
    tn = 1536
    return pl.pallas_call(
        _mod_kernel,
        grid=(depth, d6 // tn),
        in_specs=[
            pl.BlockSpec((rows, d), lambda i, n: (0, 0)),
            pl.BlockSpec((1, d, tn), lambda i, n: (i, 0, n)),
            pl.BlockSpec((1, 1, tn), lambda i, n: (i, 0, n)),
        ],
        out_specs=pl.BlockSpec((1, rows, tn), lambda i, n: (i, 0, n)),
        out_shape=jax.ShapeDtypeStruct((depth, rows, d6), F32),
        compiler_params=_params("arbitrary", "arbitrary"),
        name="adaln_modulation",
    )(c_rows, ada_w, ada_b.reshape(depth, 1, d6))


def _kv_store(ckv, k_rope, wuk_ref, wvt_ref, k_ref, v_ref):
    k_nope = _dot(ckv, wuk_ref[...])
    v_t = _dot_nt(wvt_ref[...], ckv)
    tk = v_ref.shape[4]
    ones = jnp.ones((V_ROWS - V_HEAD, tk), BF16)
    for hd in range(N_HEADS):
        k_ref[0, hd, :, 0:QK_NOPE] = k_nope[:, hd * QK_NOPE:(hd + 1) * QK_NOPE].astype(BF16)
        k_ref[0, hd, :, QK_NOPE:QK_DIM] = k_rope
        for cc in range(v_ref.shape[2]):
            v_ref[0, hd, cc, 0:V_HEAD, :] = v_t[hd * V_HEAD:(hd + 1) * V_HEAD, cc * tk:(cc + 1) * tk].astype(BF16)
            v_ref[0, hd, cc, V_HEAD:V_ROWS, :] = ones


def _qkv_latent_kernel(x_ref, mod_ref, nw_ref, qn_ref, kvn_ref, tc_ref, ts_ref, tct_ref, tst_ref,
                       wdn_ref, wuqt_ref, wuk_ref, wvt_ref, q_ref, k_ref, v_ref):
    shift, scale = mod_ref[0, 0:1, :], mod_ref[0, 1:2, :]
    h = _rms_mod(x_ref[0], nw_ref[...], shift, scale).astype(BF16)
    down = _dot(h, wdn_ref[...])
    ckv = _rms(down[:, Q_LORA:Q_LORA + KV_LORA], kvn_ref[...]).astype(BF16)
    kr0 = Q_LORA + KV_LORA
    k_rope = (down[:, kr0:kr0 + QK_ROPE] * tc_ref[...]
              + down[:, kr0 + QK_ROPE:kr0 + 2 * QK_ROPE] * ts_ref[...]).astype(BF16)
    _kv_store(ckv, k_rope, wuk_ref, wvt_ref, k_ref, v_ref)

    cq = _rms(down[:, 0:Q_LORA], qn_ref[...] * (ATTN_SCALE * LOG2_E)).astype(BF16)
    q_t = _dot_nt(wuqt_ref[...], cq)
    r0 = N_HEADS * QK_NOPE
    r1 = r0 + N_HEADS * QK_ROPE
    for pair in range(N_HEADS // 2):
        lo = 2 * QK_ROPE * pair
        rope2 = (q_t[r0 + lo:r0 + lo + 2 * QK_ROPE, :] * tct_ref[...]
                 + q_t[r1 + lo:r1 + lo + 2 * QK_ROPE, :] * tst_ref[...]).astype(BF16)
        for sub in range(2):
            hd = 2 * pair + sub
            q_ref[0, hd, 0, 0:QK_NOPE, :] = q_t[hd * QK_NOPE:(hd + 1) * QK_NOPE, :].astype(BF16)
            q_ref[0, hd, 0, QK_NOPE:QK_DIM, :] = rope2[sub * QK_ROPE:(sub + 1) * QK_ROPE, :]


def _kv_context_kernel(x_ref, mod_ref, nw_ref, kvn_ref, wdn_ref, wuk_ref, wvt_ref, k_ref, v_ref):
    shift, scale = mod_ref[0:1, :], mod_ref[1:2, :]
    h = _rms_mod(x_ref[0], nw_ref[...], shift, scale).astype(BF16)
    down = _dot(h, wdn_ref[...])
    ckv = _rms(down[:, Q_LORA:Q_LORA + KV_LORA], kvn_ref[...]).astype(BF16)
    kr0 = Q_LORA + KV_LORA
    k_rope = down[:, kr0:kr0 + QK_ROPE].astype(BF16)
    _kv_store(ckv, k_rope, wuk_ref, wvt_ref, k_ref, v_ref)


def _mla_qkv(x, ctx, mod, mod_ctx, norm_w, q_norm, kv_norm, tc, ts, tct, tst, w_down, w_uq_t, w_uk, w_v_t, tq, tk):
    b, n, d = x.shape
    n_ctx = ctx.shape[1]
    tm = tq
    assert n % tm == 0 and tm % tk == 0
    q_t, k, v_t = pl.pallas_call(
        _qkv_latent_kernel,
        grid=(b, n // tm),
        in_specs=[
            pl.BlockSpec((1, tm, d), lambda bi, j: (bi, j, 0)),
            pl.BlockSpec((1, 2, d), lambda bi, j: (bi, 0, 0)),
            _resident((1, d)),
            _resident((1, Q_LORA)),
            _resident((1, KV_LORA)),
            pl.BlockSpec((tm, QK_ROPE), lambda bi, j: (j, 0)),
            pl.BlockSpec((tm, QK_ROPE), lambda bi, j: (j, 0)),
            pl.BlockSpec((2 * QK_ROPE, tm), lambda bi, j: (0, j)),
            pl.BlockSpec((2 * QK_ROPE, tm), lambda bi, j: (0, j)),
            _resident(w_down.shape),
            _resident(w_uq_t.shape),
            _resident(w_uk.shape),
            _resident(w_v_t.shape),
        ],
        out_specs=[
            pl.BlockSpec((1, N_HEADS, 1, QK_DIM, tm), lambda bi, j: (bi, 0, j, 0, 0)),
            pl.BlockSpec((1, N_HEADS, tm, QK_DIM), lambda bi, j: (bi, 0, j, 0)),
            pl.BlockSpec((1, N_HEADS, tm // tk, V_ROWS, tk), lambda bi, j: (bi, 0, j, 0, 0)),
        ],
        out_shape=[jax.ShapeDtypeStruct((b, N_HEADS, n // tq, QK_DIM, tq), BF16),
                   jax.ShapeDtypeStruct((b, N_HEADS, n, QK_DIM), BF16),
                   jax.ShapeDtypeStruct((b, N_HEADS, n // tk, V_ROWS, tk), BF16)],
        compiler_params=_params("arbitrary", "arbitrary"),
        name="mla_qkv_latent",
    )(x, mod, norm_w, q_norm, kv_norm, tc, ts, tct, tst, w_down, w_uq_t, w_uk, w_v_t)
    k_ctx, v_ctx_t = pl.pallas_call(
        _kv_context_kernel,
        grid=(b,),
        in_specs=[
            pl.BlockSpec((1, n_ctx, d), lambda bi: (bi, 0, 0)),
            _resident((2, d)),
            _resident((1, d)),
            _resident((1, KV_LORA)),
            _resident(w_down.shape),
            _resident(w_uk.shape),
            _resident(w_v_t.shape),
        ],
        out_specs=[
            pl.BlockSpec((1, N_HEADS, n_ctx, QK_DIM), lambda bi: (bi, 0, 0, 0)),
            pl.BlockSpec((1, N_HEADS, 1, V_ROWS, n_ctx), lambda bi: (bi, 0, 0, 0, 0)),
        ],
        out_shape=[jax.ShapeDtypeStruct((b, N_HEADS, n_ctx, QK_DIM), BF16),
                   jax.ShapeDtypeStruct((b, N_HEADS, 1, V_ROWS, n_ctx), BF16)],
        compiler_params=_params("arbitrary"),
        name="mla_kv_context",
    )(ctx, mod_ctx, norm_w, kv_norm, w_down, w_uk, w_v_t)
    return q_t, k, v_t, k_ctx, v_ctx_t


def _attn_kernel(q_ref, k_ref, v_ref, kc_ref, vc_ref, o_ref, *, tk):
    q_t = q_ref[0, 0, 0]
    tq = q_t.shape[1]
    n_chunks = v_ref.shape[2]

    def scores(c):
        return _dot(k_ref[0, 0, c * tk:(c + 1) * tk, :], q_t)

    def pv(c, p):
        return _dot(v_ref[0, 0, c], p)

    def finish(acc):
        o_t = acc[0:V_HEAD, :] / acc[V_HEAD:V_HEAD + 1, :]
        o_ref[0] = o_t.T.astype(BF16)

    def context_softmax():
        s = _dot(kc_ref[0, 0], q_t)
        ref = jnp.max(s, axis=0, keepdims=True)
        return ref, jnp.exp2(s - ref).astype(BF16)

    ref, p_ctx = context_softmax()
    ahead = [scores(c) for c in range(min(2, n_chunks))]
    acc = _dot(vc_ref[0, 0, 0], p_ctx)
    acc_ref = ref
    over = jnp.zeros((1, tq), F32)
    for c in range(n_chunks):
        s = ahead.pop(0)
        if c + 2 < n_chunks:
            ahead.append(scores(c + 2))
        chunk_max = jnp.max(s, axis=0, keepdims=True)
        p = jnp.exp2(s - ref).astype(BF16)
        acc = jnp.exp2(acc_ref - ref) * acc + pv(c, p)
        acc_ref = ref
        over = jnp.maximum(over, chunk_max - ref)
        ref = jnp.maximum(ref, chunk_max)
    finish(acc)

    @pl.when(jnp.max(over) > LAG_LIMIT)
    def _():
        def body(c, carry):
            m, acc = carry
            k0 = pl.multiple_of(c * tk, tk)
            s = _dot(k_ref[0, 0, pl.ds(k0, tk), :], q_t)
            m_new = jnp.maximum(m, jnp.max(s, axis=0, keepdims=True))
            p = jnp.exp2(s - m_new).astype(BF16)
            acc = jnp.exp2(m - m_new) * acc + _dot(v_ref[0, 0, c], p)
            return m_new, acc

        m0, p0 = context_softmax()
        finish(lax.fori_loop(0, n_chunks, body, (m0, _dot(vc_ref[0, 0, 0], p0)))[1])


def _attention(q_t, k, v_t, k_ctx, v_ctx_t):
    b, nh, n_tiles, _, tq = q_t.shape
    n = n_tiles * tq
    n_chunks, tk = v_t.shape[2], v_t.shape[4]
    n_ctx = k_ctx.shape[2]
    head = lambda bi, hi, i: (bi, hi, 0, 0)
    head5 = lambda bi, hi, i: (bi, hi, 0, 0, 0)
    return pl.pallas_call(
        functools.partial(_attn_kernel, tk=tk),
        grid=(b, nh, n_tiles),
        in_specs=[
            pl.BlockSpec((1, 1, 1, QK_DIM, tq), lambda bi, hi, i: (bi, hi, i, 0, 0)),
            pl.BlockSpec((1, 1, n, QK_DIM), head),
            pl.BlockSpec((1, 1, n_chunks, V_ROWS, tk), head5),
            pl.BlockSpec((1, 1, n_ctx, QK_DIM), head),
            pl.BlockSpec((1, 1, 1, V_ROWS, n_ctx), head5),
        ],
        out_specs=pl.BlockSpec((1, tq, V_HEAD), lambda bi, hi, i: (bi, i, hi)),
        out_shape=jax.ShapeDtypeStruct((b, n, nh * V_HEAD), BF16),
        compiler_params=_params("arbitrary", "arbitrary", "arbitrary"),
        name="mla_attention",
    )(q_t, k, v_t, k_ctx, v_ctx_t)


def _ffn_kernel(*refs, with_proj, with_final):
    refs = list(refs)
    x_ref = refs.pop(0)
    o_ref = refs.pop(0) if with_proj else None
    mod_ref = refs.pop(0)
    nw_ref = refs.pop(0)
    wo_ref = refs.pop(0) if with_proj else None
    wg_ref, wu_ref, wd_ref = refs.pop(0), refs.pop(0), refs.pop(0)
    fw_ref = refs.pop(0) if with_final else None
    out_ref = refs.pop(0)

    x = x_ref[0]
    if with_proj:
        x = x + mod_ref[0, 0:1, :] * _dot(o_ref[0], wo_ref[...])
    shift, scale, gate = mod_ref[0, 1:2, :], mod_ref[0, 2:3, :], mod_ref[0, 3:4, :]
    h = _rms_mod(x, nw_ref[...], shift, scale).astype(BF16)
    g = _dot(h, wg_ref[...])
    u = _dot(h, wu_ref[...])
    act = (g * jax.nn.sigmoid(g) * u).astype(BF16)
    x = x + gate * _dot(act, wd_ref[...])
    if with_final:
        x = _rms(x, fw_ref[...])
    out_ref[0] = x


def _ffn(x, o, mod, norm_w, w_o, w_gate, w_up, w_down, final_w, tm):
    b, n, d = x.shape
    with_proj = o is not None
    with_final = final_w is not None
    tile = pl.BlockSpec((1, tm, d), lambda bi, i: (bi, i, 0))
    args, specs = [x], [tile]
    if with_proj:
        args.append(o)
        specs.append(tile)
    args += [mod, norm_w]
    specs += [pl.BlockSpec((1, 4, d), lambda bi, i: (bi, 0, 0)), _resident((1, d))]
    if with_proj:
        args.append(w_o)
        specs.append(_resident(w_o.shape))
    args += [w_gate, w_up, w_down]
    specs += [_resident(w_gate.shape), _resident(w_up.shape), _resident(w_down.shape)]
    if with_final:
        args.append(final_w)
        specs.append(_resident((1, d)))
    return pl.pallas_call(
        functools.partial(_ffn_kernel, with_proj=with_proj, with_final=with_final),
        grid=(b, n // tm),
        in_specs=specs,
        out_specs=tile,
        out_shape=jax.ShapeDtypeStruct((b, n, d), F32),
        compiler_params=_params("arbitrary", "arbitrary"),
        name="attn_proj_ffn" if with_proj else "ffn_final",
    )(*args)


def _conv_kernel(x_ref, xp_ref, xn_ref, mod_ref, nw_ref, win_ref, cw_ref, wout_ref, out_ref, z_ref):
    i = pl.program_id(1)
    last = pl.num_programs(1) - 1
    d = x_ref.shape[2]
    tm = x_ref.shape[1]
    x = x_ref[0]
    xe = jnp.concatenate([xp_ref[0], x, xn_ref[0]], axis=0)
    shift, scale, gate = mod_ref[0, 0:1, :], mod_ref[0, 1:2, :], mod_ref[0, 2:3, :]
    h = _rms_mod(xe, nw_ref[...], shift, scale).astype(BF16)
    p = _dot(h, win_ref[...])
    z = p[:, d:2 * d] * p[:, 2 * d:3 * d]
    row = lax.broadcasted_iota(jnp.int32, (tm + 2 * HALO, 1), 0)
    outside = ((row < HALO) & (i == 0)) | ((row >= tm + HALO) & (i == last))
    z_ref[...] = jnp.where(outside, 0.0, z)
    u = (cw_ref[0:1, :] * z_ref[HALO - 1:HALO - 1 + tm, :]
         + cw_ref[1:2, :] * z_ref[HALO:HALO + tm, :]
         + cw_ref[2:3, :] * z_ref[HALO + 1:HALO + 1 + tm, :])
    y = _dot((p[HALO:HALO + tm, 0:d] * u).astype(BF16), wout_ref[...])
    out_ref[0] = x + gate * y


def _short_conv(x, mod, norm_w, w_in, conv_w, w_out, tm):
    b, n, d = x.shape
    nblk = n // HALO
    per = tm // HALO
    return pl.pallas_call(
        _conv_kernel,
        grid=(b, n // tm),
        in_specs=[
            pl.BlockSpec((1, tm, d), lambda bi, i: (bi, i, 0)),
            pl.BlockSpec((1, HALO, d), lambda bi, i: (bi, jnp.maximum(i * per - 1, 0), 0)),
            pl.BlockSpec((1, HALO, d), lambda bi, i: (bi, jnp.minimum((i + 1) * per, nblk - 1), 0)),
            pl.BlockSpec((1, 3, d), lambda bi, i: (bi, 0, 0)),
            _resident((1, d)),
            _resident(w_in.shape),
            _resident(conv_w.shape),
            _resident(w_out.shape),
        ],
        out_specs=pl.BlockSpec((1, tm, d), lambda bi, i: (bi, i, 0)),
        out_shape=jax.ShapeDtypeStruct((b, n, d), F32),
        scratch_shapes=[pltpu.VMEM((tm + 2 * HALO, d), F32)],
        compiler_params=_params("arbitrary", "arbitrary"),
        name="short_conv",
    )(x, x, x, mod, norm_w, w_in, conv_w, w_out)


def _rope_tables(n):
    rows = n // GRID_W
    n_freq = QK_ROPE // 4
    freqs = ROPE_THETA ** (-jnp.arange(n_freq, dtype=F32) / n_freq)
    ar = jnp.arange(rows).astype(F32)[:, None] * freqs
    ac = jnp.arange(GRID_W).astype(F32)[:, None] * freqs

    def table(fn):
        r = jnp.broadcast_to(fn(ar)[:, None, :], (rows, GRID_W, n_freq)).reshape(n, n_freq)
        c = jnp.broadcast_to(fn(ac)[None, :, :], (rows, GRID_W, n_freq)).reshape(n, n_freq)
        return jnp.concatenate([r, r, c, c], axis=-1)

    return table(jnp.cos), table(jnp.sin)


def _rotate_cols(w):
    q = QK_ROPE // 4
    perm = np.concatenate([np.arange(q, 2 * q), np.arange(0, q), np.arange(3 * q, 4 * q), np.arange(2 * q, 3 * q)])
    sign = np.concatenate([-np.ones(q), np.ones(q), -np.ones(q), np.ones(q)]).astype(np.float32)
    return w[..., perm] * sign


def kernel(x, c, ctx, c_ctx, ada_w, ada_b, norm_mix, norm_ffn, mla_w_dq, mla_q_norm, mla_w_uq, mla_w_dkv,
           mla_kv_norm, mla_w_ukv, mla_w_o, conv_w_in, conv_w, conv_w_out, ffn_w_gate, ffn_w_up, ffn_w_down,
           final_norm):
    b, n, d = x.shape
    n_ctx = ctx.shape[1]

    pad = (-(b + 1)) % 8
    c_rows = jnp.concatenate([c, c_ctx[None, :], jnp.zeros((pad, d), F32)], axis=0)
    mods = _modulation(c_rows, ada_w, ada_b)
    m0 = mods[0, :b].reshape(b, 6, d)
    m1 = mods[1, :b].reshape(b, 6, d)
    mod_ctx = mods[0, b].reshape(6, d)[0:2]

    w_uq = mla_w_uq[0].reshape(Q_LORA, N_HEADS, QK_DIM)
    uq_rope = w_uq[:, :, QK_NOPE:]
    w_uq_t = jnp.concatenate([w_uq[:, :, :QK_NOPE].reshape(Q_LORA, -1), uq_rope.reshape(Q_LORA, -1),
                              _rotate_cols(uq_rope).reshape(Q_LORA, -1)], axis=1).T.astype(BF16)
    w_dkv = mla_w_dkv[0]
    w_down = jnp.concatenate([mla_w_dq[0], w_dkv, _rotate_cols(w_dkv[:, KV_LORA:])], axis=1).astype(BF16)
    w_ukv = mla_w_ukv[0].reshape(KV_LORA, N_HEADS, QK_NOPE + V_HEAD)
    w_uk = w_ukv[:, :, :QK_NOPE].reshape(KV_LORA, -1).astype(BF16)
    w_v_t = w_ukv[:, :, QK_NOPE:].reshape(KV_LORA, -1).T.astype(BF16)
    tc, ts = _rope_tables(n)
    tct = jnp.tile(tc.T, (2, 1))
    tst = jnp.tile(ts.T, (2, 1))

    q_t, k, v_t, k_ctx, v_ctx_t = _mla_qkv(x, ctx, m0[:, 0:2], mod_ctx, norm_mix[0][None], mla_q_norm[0][None],
                                           mla_kv_norm[0][None], tc, ts, tct, tst, w_down, w_uq_t, w_uk, w_v_t,
                                           tq=min(ATTN_TQ, n), tk=min(ATTN_TK, n))
    o = _attention(q_t, k, v_t, k_ctx, v_ctx_t)
    x = _ffn(x, o, m0[:, 2:6], norm_ffn[0][None], mla_w_o[0].astype(BF16), ffn_w_gate[0].astype(BF16),
             ffn_w_up[0].astype(BF16), ffn_w_down[0].astype(BF16), None, tm=min(512, n))

    x = _short_conv(x, m1[:, 0:3], norm_mix[1][None], conv_w_in[0].astype(BF16), conv_w[0],
                    conv_w_out[0].astype(BF16), tm=min(512, n))
    return _ffn(x, None, m1[:, 2:6], norm_ffn[1][None], None, ffn_w_gate[1].astype(BF16), ffn_w_up[1].astype(BF16),
                ffn_w_down[1].astype(BF16), final_norm[None], tm=min(512, n))
```

```python
import functools
import math

import numpy as np
import jax
import jax.numpy as jnp
from jax import lax
from jax.experimental import pallas as pl
from jax.experimental.pallas import tpu as pltpu

D_MODEL = 1024
GRID_W = 64
N_HEADS = 8
QK_NOPE = 128
QK_ROPE = 64
V_HEAD = 128
Q_LORA = 256
KV_LORA = 128
QK_DIM = QK_NOPE + QK_ROPE
V_ROWS = V_HEAD + 16
ROPE_THETA = 10000.0
NORM_EPS = 1e-6
ATTN_SCALE = 1.0 / math.sqrt(QK_NOPE + QK_ROPE)
LOG2_E = 1.4426950408889634
LAG_LIMIT = 60.0
ATTN_TQ = 1024
ATTN_TK = 512

F32 = jnp.float32
BF16 = jnp.bfloat16

VMEM_LIMIT_BYTES = 56 * 1024 * 1024
HALO = 8


def _params(*sem):
    return pltpu.CompilerParams(dimension_semantics=sem, vmem_limit_bytes=VMEM_LIMIT_BYTES)


def _resident(shape):
    zeros = (0,) * len(shape)
    return pl.BlockSpec(shape, lambda *_: zeros, pipeline_mode=pl.Buffered(1))


def _mod_spec(layer, row_of):
    return pl.BlockSpec((1, 1, 6, D_MODEL), lambda *idx: (layer, row_of(*idx), 0, 0))


def _rms(x, w):
    return x * lax.rsqrt(jnp.mean(x * x, axis=-1, keepdims=True) + NORM_EPS) * w


def _rms_mod(x, w, shift, scale):
    return x * lax.rsqrt(jnp.mean(x * x, axis=-1, keepdims=True) + NORM_EPS) * (w * (1.0 + scale)) + shift


def _dot(a, b):
    return jnp.dot(a, b, preferred_element_type=F32)


def _dot_tn(a, b):
    return lax.dot_general(a, b, (((0,), (0,)), ((), ())), preferred_element_type=F32)


def _dot_nt(a, b):
    return lax.dot_general(a, b, (((1,), (1,)), ((), ())), preferred_element_type=F32)


def _mod_kernel(c_ref, w_ref, b_ref, o_ref):
    c = c_ref[...]
    o_ref[0] = _dot(c * jax.nn.sigmoid(c), w_ref[0]) + b_ref[0]


def _modulation(c_rows, ada_w, ada_b):
    depth, d, d6 = ada_w.shape
    rows = c_rows.shape[0]
    tn = 1536
    return pl.pallas_call(
        _mod_kernel,
        grid=(depth, d6 // tn),
        in_specs=[
            pl.BlockSpec((rows, d), lambda i, n: (0, 0)),
            pl.BlockSpec((1, d, tn), lambda i, n: (i, 0, n)),
            pl.BlockSpec((1, 1, tn), lambda i, n: (i, 0, n)),
        ],
        out_specs=pl.BlockSpec((1, rows, tn), lambda i, n: (i, 0, n)),
        out_shape=jax.ShapeDtypeStruct((depth, rows, d6), F32),
        compiler_params=_params("arbitrary", "arbitrary"),
        name="adaln_modulation",
    )(c_rows, ada_w, ada_b.reshape(depth, 1, d6))


def _kv_store(ckv, k_rope, wuk_ref, wvt_ref, k_ref, v_ref):
    k_nope = _dot(ckv, wuk_ref[...])
    v_t = _dot_nt(wvt_ref[...], ckv)
    tk = v_ref.shape[4]
    ones = jnp.ones((V_ROWS - V_HEAD, tk), BF16)
    for hd in range(N_HEADS):
        k_ref[0, hd, :, 0:QK_NOPE] = k_nope[:, hd * QK_NOPE:(hd + 1) * QK_NOPE].astype(BF16)
        k_ref[0, hd, :, QK_NOPE:QK_DIM] = k_rope
        for cc in range(v_ref.shape[2]):
            v_ref[0, hd, cc, 0:V_HEAD, :] = v_t[hd * V_HEAD:(hd + 1) * V_HEAD, cc * tk:(cc + 1) * tk].astype(BF16)
            v_ref[0, hd, cc, V_HEAD:V_ROWS, :] = ones


def _qkv_latent_kernel(x_ref, mod_ref, nw_ref, qn_ref, kvn_ref, tc_ref, ts_ref, tct_ref, tst_ref,
                       wdn_ref, wuqt_ref, wuk_ref, wvt_ref, q_ref, k_ref, v_ref):
    shift, scale = mod_ref[0, 0, 0:1, :], mod_ref[0, 0, 1:2, :]
    h = _rms_mod(x_ref[0], nw_ref[...], shift, scale).astype(BF16)
    down = _dot(h, wdn_ref[...])
    ckv = _rms(down[:, Q_LORA:Q_LORA + KV_LORA], kvn_ref[...]).astype(BF16)
    kr0 = Q_LORA + KV_LORA
    k_rope = (down[:, kr0:kr0 + QK_ROPE] * tc_ref[...]
              + down[:, kr0 + QK_ROPE:kr0 + 2 * QK_ROPE] * ts_ref[...]).astype(BF16)
    _kv_store(ckv, k_rope, wuk_ref, wvt_ref, k_ref, v_ref)

    cq = _rms(down[:, 0:Q_LORA], qn_ref[...] * (ATTN_SCALE * LOG2_E)).astype(BF16)
    q_t = _dot_nt(wuqt_ref[...], cq)
    r0 = N_HEADS * QK_NOPE
    r1 = r0 + N_HEADS * QK_ROPE
    cos2 = jnp.concatenate([tct_ref[...], tct_ref[...]], axis=0)
    sin2 = jnp.concatenate([tst_ref[...], tst_ref[...]], axis=0)
    for pair in range(N_HEADS // 2):
        lo = 2 * QK_ROPE * pair
        rope2 = (q_t[r0 + lo:r0 + lo + 2 * QK_ROPE, :] * cos2
                 + q_t[r1 + lo:r1 + lo + 2 * QK_ROPE, :] * sin2).astype(BF16)
        for sub in range(2):
            hd = 2 * pair + sub
            q_ref[0, hd, 0, 0:QK_NOPE, :] = q_t[hd * QK_NOPE:(hd + 1) * QK_NOPE, :].astype(BF16)
            q_ref[0, hd, 0, QK_NOPE:QK_DIM, :] = rope2[sub * QK_ROPE:(sub + 1) * QK_ROPE, :]


def _kv_context_kernel(x_ref, mod_ref, nw_ref, kvn_ref, wdn_ref, wuk_ref, wvt_ref, k_ref, v_ref):
    shift, scale = mod_ref[0, 0, 0:1, :], mod_ref[0, 0, 1:2, :]
    h = _rms_mod(x_ref[0], nw_ref[...], shift, scale).astype(BF16)
    down = _dot(h, wdn_ref[...])
    ckv = _rms(down[:, Q_LORA:Q_LORA + KV_LORA], kvn_ref[...]).astype(BF16)
    kr0 = Q_LORA + KV_LORA
    k_rope = down[:, kr0:kr0 + QK_ROPE].astype(BF16)
    _kv_store(ckv, k_rope, wuk_ref, wvt_ref, k_ref, v_ref)


def _mla_qkv(x, ctx, mods, norm_w, q_norm, kv_norm, tc, ts, tct, tst, w_down, w_uq_t, w_uk, w_v_t, tq, tk):
    b, n, d = x.shape
    n_ctx = ctx.shape[1]
    tm = tq
    assert n % tm == 0 and tm % tk == 0
    q_t, k, v_t = pl.pallas_call(
        _qkv_latent_kernel,
        grid=(b, n // tm),
        in_specs=[
            pl.BlockSpec((1, tm, d), lambda bi, j: (bi, j, 0)),
            _mod_spec(0, lambda bi, j: bi),
            _resident((1, d)),
            _resident((1, Q_LORA)),
            _resident((1, KV_LORA)),
            pl.BlockSpec((tm, QK_ROPE), lambda bi, j: (j, 0)),
            pl.BlockSpec((tm, QK_ROPE), lambda bi, j: (j, 0)),
            pl.BlockSpec((QK_ROPE, tm), lambda bi, j: (0, j)),
            pl.BlockSpec((QK_ROPE, tm), lambda bi, j: (0, j)),
            _resident(w_down.shape),
            _resident(w_uq_t.shape),
            _resident(w_uk.shape),
            _resident(w_v_t.shape),
        ],
        out_specs=[
            pl.BlockSpec((1, N_HEADS, 1, QK_DIM, tm), lambda bi, j: (bi, 0, j, 0, 0)),
            pl.BlockSpec((1, N_HEADS, tm, QK_DIM), lambda bi, j: (bi, 0, j, 0)),
            pl.BlockSpec((1, N_HEADS, tm // tk, V_ROWS, tk), lambda bi, j: (bi, 0, j, 0, 0)),
        ],
        out_shape=[jax.ShapeDtypeStruct((b, N_HEADS, n // tq, QK_DIM, tq), BF16),
                   jax.ShapeDtypeStruct((b, N_HEADS, n, QK_DIM), BF16),
                   jax.ShapeDtypeStruct((b, N_HEADS, n // tk, V_ROWS, tk), BF16)],
        compiler_params=_params("arbitrary", "arbitrary"),
        name="mla_qkv_latent",
    )(x, mods, norm_w, q_norm, kv_norm, tc, ts, tct, tst, w_down, w_uq_t, w_uk, w_v_t)
    k_ctx, v_ctx_t = pl.pallas_call(
        _kv_context_kernel,
        grid=(b,),
        in_specs=[
            pl.BlockSpec((1, n_ctx, d), lambda bi: (bi, 0, 0)),
            _mod_spec(0, lambda bi: b),
            _resident((1, d)),
            _resident((1, KV_LORA)),
            _resident(w_down.shape),
            _resident(w_uk.shape),
            _resident(w_v_t.shape),
        ],
        out_specs=[
            pl.BlockSpec((1, N_HEADS, n_ctx, QK_DIM), lambda bi: (bi, 0, 0, 0)),
            pl.BlockSpec((1, N_HEADS, 1, V_ROWS, n_ctx), lambda bi: (bi, 0, 0, 0, 0)),
        ],
        out_shape=[jax.ShapeDtypeStruct((b, N_HEADS, n_ctx, QK_DIM), BF16),
                   jax.ShapeDtypeStruct((b, N_HEADS, 1, V_ROWS, n_ctx), BF16)],
        compiler_params=_params("arbitrary"),
        name="mla_kv_context",
    )(ctx, mods, norm_w, kv_norm, w_down, w_uk, w_v_t)
    return q_t, k, v_t, k_ctx, v_ctx_t


def _attn_kernel(q_ref, k_ref, v_ref, kc_ref, vc_ref, o_ref, *, tk):
    q_t = q_ref[0, 0, 0]
    tq = q_t.shape[1]
    n_chunks = v_ref.shape[2]

    def scores(c):
        return _dot(k_ref[0, 0, c * tk:(c + 1) * tk, :], q_t)

    def pv(c, p):
        return _dot(v_ref[0, 0, c], p)

    def finish(acc):
        o_ref[0] = (acc[0:V_HEAD, :] / acc[V_HEAD:V_HEAD + 1, :]).astype(BF16)

    def context_softmax():
        s = _dot(kc_ref[0, 0], q_t)
        ref = jnp.max(s, axis=0, keepdims=True)
        return ref, jnp.exp2(s - ref).astype(BF16)

    ref, p_ctx = context_softmax()
    ahead = [scores(c) for c in range(min(2, n_chunks))]
    acc = _dot(vc_ref[0, 0, 0], p_ctx)
    acc_ref = ref
    over = jnp.zeros((1, tq), F32)
    for c in range(n_chunks):
        s = ahead.pop(0)
        if c + 2 < n_chunks:
            ahead.append(scores(c + 2))
        chunk_max = jnp.max(s, axis=0, keepdims=True)
        p = jnp.exp2(s - ref).astype(BF16)
        acc = jnp.exp2(acc_ref - ref) * acc + pv(c, p)
        acc_ref = ref
        over = jnp.maximum(over, chunk_max - ref)
        ref = jnp.maximum(ref, chunk_max)
    finish(acc)

    @pl.when(jnp.max(over) > LAG_LIMIT)
    def _():
        def body(c, carry):
            m, acc = carry
            k0 = pl.multiple_of(c * tk, tk)
            s = _dot(k_ref[0, 0, pl.ds(k0, tk), :], q_t)
            m_new = jnp.maximum(m, jnp.max(s, axis=0, keepdims=True))
            p = jnp.exp2(s - m_new).astype(BF16)
            acc = jnp.exp2(m - m_new) * acc + _dot(v_ref[0, 0, c], p)
            return m_new, acc

        m0, p0 = context_softmax()
        finish(lax.fori_loop(0, n_chunks, body, (m0, _dot(vc_ref[0, 0, 0], p0)))[1])


def _attention(q_t, k, v_t, k_ctx, v_ctx_t):
    b, nh, n_tiles, _, tq = q_t.shape
    n = n_tiles * tq
    n_chunks, tk = v_t.shape[2], v_t.shape[4]
    n_ctx = k_ctx.shape[2]
    head = lambda bi, hi, i: (bi, hi, 0, 0)
    head5 = lambda bi, hi, i: (bi, hi, 0, 0, 0)
    return pl.pallas_call(
        functools.partial(_attn_kernel, tk=tk),
        grid=(b, nh, n_tiles),
        in_specs=[
            pl.BlockSpec((1, 1, 1, QK_DIM, tq), lambda bi, hi, i: (bi, hi, i, 0, 0)),
            pl.BlockSpec((1, 1, n, QK_DIM), head),
            pl.BlockSpec((1, 1, n_chunks, V_ROWS, tk), head5),
            pl.BlockSpec((1, 1, n_ctx, QK_DIM), head),
            pl.BlockSpec((1, 1, 1, V_ROWS, n_ctx), head5),
        ],
        out_specs=pl.BlockSpec((1, V_HEAD, tq), lambda bi, hi, i: (bi, hi, i)),
        out_shape=jax.ShapeDtypeStruct((b, nh * V_HEAD, n), BF16),
        compiler_params=_params("arbitrary", "arbitrary", "arbitrary"),
        name="mla_attention",
    )(q_t, k, v_t, k_ctx, v_ctx_t)


def _ffn_kernel(*refs, with_proj, with_final):
    refs = list(refs)
    x_ref = refs.pop(0)
    o_ref = refs.pop(0) if with_proj else None
    mod_ref = refs.pop(0)
    nw_ref = refs.pop(0)
    wo_ref = refs.pop(0) if with_proj else None
    wg_ref, wu_ref, wd_ref = refs.pop(0), refs.pop(0), refs.pop(0)
    fw_ref = refs.pop(0) if with_final else None
    out_ref = refs.pop(0)

    x = x_ref[0]
    if with_proj:
        x = x + mod_ref[0, 0, 2:3, :] * _dot_tn(o_ref[0], wo_ref[...])
    shift, scale, gate = mod_ref[0, 0, 3:4, :], mod_ref[0, 0, 4:5, :], mod_ref[0, 0, 5:6, :]
    h = _rms_mod(x, nw_ref[...], shift, scale).astype(BF16)
    g = _dot(h, wg_ref[...])
    u = _dot(h, wu_ref[...])
    act = (g * jax.nn.sigmoid(g) * u).astype(BF16)
    x = x + gate * _dot(act, wd_ref[...])
    if with_final:
        x = _rms(x, fw_ref[...])
    out_ref[0] = x


def _ffn(x, o_t, mods, layer, norm_w, w_o, w_gate, w_up, w_down, final_w, tm):
    b, n, d = x.shape
    with_proj = o_t is not None
    with_final = final_w is not None
    tile = pl.BlockSpec((1, tm, d), lambda bi, i: (bi, i, 0))
    args, specs = [x], [tile]
    if with_proj:
        args.append(o_t)
        specs.append(pl.BlockSpec((1, d, tm), lambda bi, i: (bi, 0, i)))
    args += [mods, norm_w]
    specs += [_mod_spec(layer, lambda bi, i: bi), _resident((1, d))]
    if with_proj:
        args.append(w_o)
        specs.append(_resident(w_o.shape))
    args += [w_gate, w_up, w_down]
    specs += [_resident(w_gate.shape), _resident(w_up.shape), _resident(w_down.shape)]
    if with_final:
        args.append(final_w)
        specs.append(_resident((1, d)))
    return pl.pallas_call(
        functools.partial(_ffn_kernel, with_proj=with_proj, with_final=with_final),
        grid=(b, n // tm),
        in_specs=specs,
        out_specs=tile,
        out_shape=jax.ShapeDtypeStruct((b, n, d), F32),
        compiler_params=_params("arbitrary", "arbitrary"),
        name="attn_proj_ffn" if with_proj else "ffn_final",
    )(*args)


def _conv_kernel(x_ref, xp_ref, xn_ref, mod_ref, nw_ref, win_ref, cw_ref, wout_ref, out_ref, z_ref):
    i = pl.program_id(1)
    last = pl.num_programs(1) - 1
    d = x_ref.shape[2]
    tm = x_ref.shape[1]
    x = x_ref[0]
    xe = jnp.concatenate([xp_ref[0], x, xn_ref[0]], axis=0)
    shift, scale, gate = mod_ref[0, 0, 0:1, :], mod_ref[0, 0, 1:2, :], mod_ref[0, 0, 2:3, :]
    h = _rms_mod(xe, nw_ref[...], shift, scale).astype(BF16)
    p = _dot(h, win_ref[...])
    z = p[:, d:2 * d] * p[:, 2 * d:3 * d]
    row = lax.broadcasted_iota(jnp.int32, (tm + 2 * HALO, 1), 0)
    outside = ((row < HALO) & (i == 0)) | ((row >= tm + HALO) & (i == last))
    z_ref[...] = jnp.where(outside, 0.0, z)
    u = (cw_ref[0:1, :] * z_ref[HALO - 1:HALO - 1 + tm, :]
         + cw_ref[1:2, :] * z_ref[HALO:HALO + tm, :]
         + cw_ref[2:3, :] * z_ref[HALO + 1:HALO + 1 + tm, :])
    y = _dot((p[HALO:HALO + tm, 0:d] * u).astype(BF16), wout_ref[...])
    out_ref[0] = x + gate * y


def _short_conv(x, mods, layer, norm_w, w_in, conv_w, w_out, tm):
    b, n, d = x.shape
    nblk = n // HALO
    per = tm // HALO
    return pl.pallas_call(
        _conv_kernel,
        grid=(b, n // tm),
        in_specs=[
            pl.BlockSpec((1, tm, d), lambda bi, i: (bi, i, 0)),
            pl.BlockSpec((1, HALO, d), lambda bi, i: (bi, jnp.maximum(i * per - 1, 0), 0)),
            pl.BlockSpec((1, HALO, d), lambda bi, i: (bi, jnp.minimum((i + 1) * per, nblk - 1), 0)),
            _mod_spec(layer, lambda bi, i: bi),
            _resident((1, d)),
            _resident(w_in.shape),
            _resident(conv_w.shape),
            _resident(w_out.shape),
        ],
        out_specs=pl.BlockSpec((1, tm, d), lambda bi, i: (bi, i, 0)),
        out_shape=jax.ShapeDtypeStruct((b, n, d), F32),
        scratch_shapes=[pltpu.VMEM((tm + 2 * HALO, d), F32)],
        compiler_params=_params("arbitrary", "arbitrary"),
        name="short_conv",
    )(x, x, x, mods, norm_w, w_in, conv_w, w_out)


def _rope_tables(n):
    rows = n // GRID_W
    n_freq = QK_ROPE // 4
    freqs = ROPE_THETA ** (-jnp.arange(n_freq, dtype=F32) / n_freq)
    ar = jnp.arange(rows).astype(F32)[:, None] * freqs
    ac = jnp.arange(GRID_W).astype(F32)[:, None] * freqs

    def table(fn):
        r = jnp.broadcast_to(fn(ar)[:, None, :], (rows, GRID_W, n_freq)).reshape(n, n_freq)
        c = jnp.broadcast_to(fn(ac)[None, :, :], (rows, GRID_W, n_freq)).reshape(n, n_freq)
        return jnp.concatenate([r, r, c, c], axis=-1)

    return table(jnp.cos), table(jnp.sin)


def _rotate_cols(w):
    q = QK_ROPE // 4
    perm = np.concatenate([np.arange(q, 2 * q), np.arange(0, q), np.arange(3 * q, 4 * q), np.arange(2 * q, 3 * q)])
    sign = np.concatenate([-np.ones(q), np.ones(q), -np.ones(q), np.ones(q)]).astype(np.float32)
    return w[..., perm] * sign


def kernel(x, c, ctx, c_ctx, ada_w, ada_b, norm_mix, norm_ffn, mla_w_dq, mla_q_norm, mla_w_uq, mla_w_dkv,
           mla_kv_norm, mla_w_ukv, mla_w_o, conv_w_in, conv_w, conv_w_out, ffn_w_gate, ffn_w_up, ffn_w_down,
           final_norm):
    b, n, d = x.shape

    pad = (-(b + 1)) % 8
    c_rows = jnp.concatenate([c, c_ctx[None, :], jnp.zeros((pad, d), F32)], axis=0)
    mods = _modulation(c_rows, ada_w, ada_b)
    mods = mods.reshape(mods.shape[0], mods.shape[1], 6, d)

    w_uq = mla_w_uq[0].reshape(Q_LORA, N_HEADS, QK_DIM)
    uq_rope = w_uq[:, :, QK_NOPE:]
    w_uq_t = jnp.concatenate([w_uq[:, :, :QK_NOPE].reshape(Q_LORA, -1), uq_rope.reshape(Q_LORA, -1),
                              _rotate_cols(uq_rope).reshape(Q_LORA, -1)], axis=1).T.astype(BF16)
    w_dkv = mla_w_dkv[0]
    w_down = jnp.concatenate([mla_w_dq[0], w_dkv, _rotate_cols(w_dkv[:, KV_LORA:])], axis=1).astype(BF16)
    w_ukv = mla_w_ukv[0].reshape(KV_LORA, N_HEADS, QK_NOPE + V_HEAD)
    w_uk = w_ukv[:, :, :QK_NOPE].reshape(KV_LORA, -1).astype(BF16)
    w_v_t = w_ukv[:, :, QK_NOPE:].reshape(KV_LORA, -1).T.astype(BF16)
    tc, ts = _rope_tables(n)
    tct, tst = tc.T, ts.T

    q_t, k, v_t, k_ctx, v_ctx_t = _mla_qkv(x, ctx, mods, norm_mix[0][None], mla_q_norm[0][None],
                                           mla_kv_norm[0][None], tc, ts, tct, tst, w_down, w_uq_t, w_uk, w_v_t,
                                           tq=min(ATTN_TQ, n), tk=min(ATTN_TK, n))
    o_t = _attention(q_t, k, v_t, k_ctx, v_ctx_t)
    x = _ffn(x, o_t, mods, 0, norm_ffn[0][None], mla_w_o[0].astype(BF16), ffn_w_gate[0].astype(BF16),
             ffn_w_up[0].astype(BF16), ffn_w_down[0].astype(BF16), None, tm=min(512, n))

    x = _short_conv(x, mods, 1, norm_mix[1][None], conv_w_in[0].astype(BF16), conv_w[0],
                    conv_w_out[0].astype(BF16), tm=min(512, n))
    return _ffn(x, None, mods, 1, norm_ffn[1][None], None, ffn_w_gate[1].astype(BF16), ffn_w_up[1].astype(BF16),
                ffn_w_down[1].astype(BF16), final_norm[None], tm=min(512, n))
```

```python
import functools
import math

import numpy as np
import jax
import jax.numpy as jnp
from jax import lax
from jax.experimental import pallas as pl
from jax.experimental.pallas import tpu as pltpu

D_MODEL = 1024
GRID_W = 64
N_HEADS = 8
QK_NOPE = 128
QK_ROPE = 64
V_HEAD = 128
Q_LORA = 256
KV_LORA = 128
QK_DIM = QK_NOPE + QK_ROPE
V_ROWS = V_HEAD + 16
ROPE_THETA = 10000.0
NORM_EPS = 1e-6
ATTN_SCALE = 1.0 / math.sqrt(QK_NOPE + QK_ROPE)
LOG2_E = 1.4426950408889634
LAG_LIMIT = 60.0
ATTN_TQ = 1024
ATTN_TK = 512

F32 = jnp.float32
BF16 = jnp.bfloat16

VMEM_LIMIT_BYTES = 56 * 1024 * 1024
HALO = 8


def _params(*sem):
    return pltpu.CompilerParams(dimension_semantics=sem, vmem_limit_bytes=VMEM_LIMIT_BYTES)


def _resident(shape):
    zeros = (0,) * len(shape)
    return pl.BlockSpec(shape, lambda *_: zeros, pipeline_mode=pl.Buffered(1))


def _resident_layer(shape, layer):
    rest = (0,) * (len(shape) - 1)
    return pl.BlockSpec((1,) + tuple(shape[1:]), lambda *_: (layer,) + rest, pipeline_mode=pl.Buffered(1))


def _mod_spec(layer, row_of):
    return pl.BlockSpec((1, 1, 6, D_MODEL), lambda *idx: (layer, row_of(*idx), 0, 0))


def _rms(x, w):
    return x * lax.rsqrt(jnp.mean(x * x, axis=-1, keepdims=True) + NORM_EPS) * w


def _rms_mod(x, w, shift, scale):
    return x * lax.rsqrt(jnp.mean(x * x, axis=-1, keepdims=True) + NORM_EPS) * (w * (1.0 + scale)) + shift


def _dot(a, b):
    return jnp.dot(a, b, preferred_element_type=F32)


def _dot_tn(a, b):
    return lax.dot_general(a, b, (((0,), (0,)), ((), ())), preferred_element_type=F32)


def _dot_nt(a, b):
    return lax.dot_general(a, b, (((1,), (1,)), ((), ())), preferred_element_type=F32)


def _mod_kernel(c_ref, w_ref, b_ref, o_ref):
    c = c_ref[...]
    o_ref[0] = _dot(c * jax.nn.sigmoid(c), w_ref[0]) + b_ref[0]


def _modulation(c_rows, ada_w, ada_b):
    depth, d, d6 = ada_w.shape
    rows = c_rows.shape[0]
    tn = 1536
    return pl.pallas_call(
        _mod_kernel,
        grid=(depth, d6 // tn),
        in_specs=[
            pl.BlockSpec((rows, d), lambda i, n: (0, 0)),
            pl.BlockSpec((1, d, tn), lambda i, n: (i, 0, n)),
            pl.BlockSpec((1, 1, tn), lambda i, n: (i, 0, n)),
        ],
        out_specs=pl.BlockSpec((1, rows, tn), lambda i, n: (i, 0, n)),
        out_shape=jax.ShapeDtypeStruct((depth, rows, d6), F32),
        compiler_params=_params("arbitrary", "arbitrary"),
        name="adaln_modulation",
    )(c_rows, ada_w, ada_b.reshape(depth, 1, d6))


def _kv_store(ckv, k_rope, wuk_ref, wvt_ref, k_ref, v_ref):
    k_nope = _dot(ckv, wuk_ref[...])
    v_t = _dot_nt(wvt_ref[...], ckv)
    tk = v_ref.shape[4]
    ones = jnp.ones((V_ROWS - V_HEAD, tk), BF16)
    for hd in range(N_HEADS):
        k_ref[0, hd, :, 0:QK_NOPE] = k_nope[:, hd * QK_NOPE:(hd + 1) * QK_NOPE].astype(BF16)
        k_ref[0, hd, :, QK_NOPE:QK_DIM] = k_rope
        for cc in range(v_ref.shape[2]):
            v_ref[0, hd, cc, 0:V_HEAD, :] = v_t[hd * V_HEAD:(hd + 1) * V_HEAD, cc * tk:(cc + 1) * tk].astype(BF16)
            v_ref[0, hd, cc, V_HEAD:V_ROWS, :] = ones


def _qkv_latent_kernel(x_ref, mod_ref, nw_ref, qn_ref, kvn_ref, tc_ref, ts_ref, tct_ref, tst_ref,
                       wdn_ref, wuqt_ref, wuk_ref, wvt_ref, q_ref, k_ref, v_ref):
    shift, scale = mod_ref[0, 0, 0:1, :], mod_ref[0, 0, 1:2, :]
    h = _rms_mod(x_ref[0], nw_ref[...], shift, scale).astype(BF16)
    down = _dot(h, wdn_ref[...])
    ckv = _rms(down[:, Q_LORA:Q_LORA + KV_LORA], kvn_ref[...]).astype(BF16)
    kr0 = Q_LORA + KV_LORA
    k_rope = (down[:, kr0:kr0 + QK_ROPE] * tc_ref[...]
              + down[:, kr0 + QK_ROPE:kr0 + 2 * QK_ROPE] * ts_ref[...]).astype(BF16)
    _kv_store(ckv, k_rope, wuk_ref, wvt_ref, k_ref, v_ref)

    cq = _rms(down[:, 0:Q_LORA], qn_ref[...] * (ATTN_SCALE * LOG2_E)).astype(BF16)
    q_t = _dot_nt(wuqt_ref[...], cq)
    r0 = N_HEADS * QK_NOPE
    r1 = r0 + N_HEADS * QK_ROPE
    cos2 = jnp.concatenate([tct_ref[...], tct_ref[...]], axis=0)
    sin2 = jnp.concatenate([tst_ref[...], tst_ref[...]], axis=0)
    for pair in range(N_HEADS // 2):
        lo = 2 * QK_ROPE * pair
        rope2 = (q_t[r0 + lo:r0 + lo + 2 * QK_ROPE, :] * cos2
                 + q_t[r1 + lo:r1 + lo + 2 * QK_ROPE, :] * sin2).astype(BF16)
        for sub in range(2):
            hd = 2 * pair + sub
            q_ref[0, hd, 0, 0:QK_NOPE, :] = q_t[hd * QK_NOPE:(hd + 1) * QK_NOPE, :].astype(BF16)
            q_ref[0, hd, 0, QK_NOPE:QK_DIM, :] = rope2[sub * QK_ROPE:(sub + 1) * QK_ROPE, :]


def _kv_context_kernel(x_ref, mod_ref, nw_ref, kvn_ref, wdn_ref, wuk_ref, wvt_ref, k_ref, v_ref):
    shift, scale = mod_ref[0, 0, 0:1, :], mod_ref[0, 0, 1:2, :]
    h = _rms_mod(x_ref[0], nw_ref[...], shift, scale).astype(BF16)
    down = _dot(h, wdn_ref[...])
    ckv = _rms(down[:, Q_LORA:Q_LORA + KV_LORA], kvn_ref[...]).astype(BF16)
    kr0 = Q_LORA + KV_LORA
    k_rope = down[:, kr0:kr0 + QK_ROPE].astype(BF16)
    _kv_store(ckv, k_rope, wuk_ref, wvt_ref, k_ref, v_ref)


def _mla_qkv(x, ctx, mods, norm_w, q_norm, kv_norm, tc, ts, tct, tst, w_down, w_uq_t, w_uk, w_v_t, tq, tk):
    b, n, d = x.shape
    n_ctx = ctx.shape[1]
    tm = tq
    assert n % tm == 0 and tm % tk == 0
    q_t, k, v_t = pl.pallas_call(
        _qkv_latent_kernel,
        grid=(n // tm, b),
        in_specs=[
            pl.BlockSpec((1, tm, d), lambda j, bi: (bi, j, 0)),
            _mod_spec(0, lambda j, bi: bi),
            _resident((1, d)),
            _resident((1, Q_LORA)),
            _resident((1, KV_LORA)),
            pl.BlockSpec((tm, QK_ROPE), lambda j, bi: (j, 0)),
            pl.BlockSpec((tm, QK_ROPE), lambda j, bi: (j, 0)),
            pl.BlockSpec((QK_ROPE, tm), lambda j, bi: (0, j)),
            pl.BlockSpec((QK_ROPE, tm), lambda j, bi: (0, j)),
            _resident(w_down.shape),
            _resident(w_uq_t.shape),
            _resident(w_uk.shape),
            _resident(w_v_t.shape),
        ],
        out_specs=[
            pl.BlockSpec((1, N_HEADS, 1, QK_DIM, tm), lambda j, bi: (bi, 0, j, 0, 0)),
            pl.BlockSpec((1, N_HEADS, tm, QK_DIM), lambda j, bi: (bi, 0, j, 0)),
            pl.BlockSpec((1, N_HEADS, tm // tk, V_ROWS, tk), lambda j, bi: (bi, 0, j, 0, 0)),
        ],
        out_shape=[jax.ShapeDtypeStruct((b, N_HEADS, n // tq, QK_DIM, tq), BF16),
                   jax.ShapeDtypeStruct((b, N_HEADS, n, QK_DIM), BF16),
                   jax.ShapeDtypeStruct((b, N_HEADS, n // tk, V_ROWS, tk), BF16)],
        compiler_params=_params("arbitrary", "arbitrary"),
        name="mla_qkv_latent",
    )(x, mods, norm_w, q_norm, kv_norm, tc, ts, tct, tst, w_down, w_uq_t, w_uk, w_v_t)
    k_ctx, v_ctx_t = pl.pallas_call(
        _kv_context_kernel,
        grid=(b,),
        in_specs=[
            pl.BlockSpec((1, n_ctx, d), lambda bi: (bi, 0, 0)),
            _mod_spec(0, lambda bi: b),
            _resident((1, d)),
            _resident((1, KV_LORA)),
            _resident(w_down.shape),
            _resident(w_uk.shape),
            _resident(w_v_t.shape),
        ],
        out_specs=[
            pl.BlockSpec((1, N_HEADS, n_ctx, QK_DIM), lambda bi: (bi, 0, 0, 0)),
            pl.BlockSpec((1, N_HEADS, 1, V_ROWS, n_ctx), lambda bi: (bi, 0, 0, 0, 0)),
        ],
        out_shape=[jax.ShapeDtypeStruct((b, N_HEADS, n_ctx, QK_DIM), BF16),
                   jax.ShapeDtypeStruct((b, N_HEADS, 1, V_ROWS, n_ctx), BF16)],
        compiler_params=_params("arbitrary"),
        name="mla_kv_context",
    )(ctx, mods, norm_w, kv_norm, w_down, w_uk, w_v_t)
    return q_t, k, v_t, k_ctx, v_ctx_t


def _attn_kernel(q_ref, k_ref, v_ref, kc_ref, vc_ref, o_ref, *, tk):
    q_t = q_ref[0, 0, 0]
    tq = q_t.shape[1]
    n_chunks = v_ref.shape[2]

    def scores(c):
        return _dot(k_ref[0, 0, c * tk:(c + 1) * tk, :], q_t)

    def pv(c, p):
        return _dot(v_ref[0, 0, c], p)

    def finish(acc):
        o_ref[0] = (acc[0:V_HEAD, :] / acc[V_HEAD:V_HEAD + 1, :]).astype(BF16)

    def context_softmax():
        s = _dot(kc_ref[0, 0], q_t)
        ref = jnp.max(s, axis=0, keepdims=True)
        return ref, jnp.exp2(s - ref).astype(BF16)

    ref, p_ctx = context_softmax()
    ahead = [scores(c) for c in range(min(2, n_chunks))]
    acc = _dot(vc_ref[0, 0, 0], p_ctx)
    acc_ref = ref
    over = jnp.zeros((1, tq), F32)
    for c in range(n_chunks):
        s = ahead.pop(0)
        if c + 2 < n_chunks:
            ahead.append(scores(c + 2))
        chunk_max = jnp.max(s, axis=0, keepdims=True)
        p = jnp.exp2(s - ref).astype(BF16)
        acc = jnp.exp2(acc_ref - ref) * acc + pv(c, p)
        acc_ref = ref
        over = jnp.maximum(over, chunk_max - ref)
        ref = jnp.maximum(ref, chunk_max)
    finish(acc)

    @pl.when(jnp.max(over) > LAG_LIMIT)
    def _():
        def body(c, carry):
            m, acc = carry
            k0 = pl.multiple_of(c * tk, tk)
            s = _dot(k_ref[0, 0, pl.ds(k0, tk), :], q_t)
            m_new = jnp.maximum(m, jnp.max(s, axis=0, keepdims=True))
            p = jnp.exp2(s - m_new).astype(BF16)
            acc = jnp.exp2(m - m_new) * acc + _dot(v_ref[0, 0, c], p)
            return m_new, acc

        m0, p0 = context_softmax()
        finish(lax.fori_loop(0, n_chunks, body, (m0, _dot(vc_ref[0, 0, 0], p0)))[1])


def _attention(q_t, k, v_t, k_ctx, v_ctx_t):
    b, nh, n_tiles, _, tq = q_t.shape
    n = n_tiles * tq
    n_chunks, tk = v_t.shape[2], v_t.shape[4]
    n_ctx = k_ctx.shape[2]
    head = lambda bi, hi, i: (bi, hi, 0, 0)
    head5 = lambda bi, hi, i: (bi, hi, 0, 0, 0)
    return pl.pallas_call(
        functools.partial(_attn_kernel, tk=tk),
        grid=(b, nh, n_tiles),
        in_specs=[
            pl.BlockSpec((1, 1, 1, QK_DIM, tq), lambda bi, hi, i: (bi, hi, i, 0, 0)),
            pl.BlockSpec((1, 1, n, QK_DIM), head),
            pl.BlockSpec((1, 1, n_chunks, V_ROWS, tk), head5),
            pl.BlockSpec((1, 1, n_ctx, QK_DIM), head),
            pl.BlockSpec((1, 1, 1, V_ROWS, n_ctx), head5),
        ],
        out_specs=pl.BlockSpec((1, V_HEAD, tq), lambda bi, hi, i: (bi, hi, i)),
        out_shape=jax.ShapeDtypeStruct((b, nh * V_HEAD, n), BF16),
        compiler_params=_params("arbitrary", "arbitrary", "arbitrary"),
        name="mla_attention",
    )(q_t, k, v_t, k_ctx, v_ctx_t)


def _ffn_kernel(*refs, with_proj, with_final):
    refs = list(refs)
    x_ref = refs.pop(0)
    o_ref = refs.pop(0) if with_proj else None
    mod_ref = refs.pop(0)
    nw_ref = refs.pop(0)
    wo_ref = refs.pop(0) if with_proj else None
    wg_ref, wu_ref, wd_ref = refs.pop(0), refs.pop(0), refs.pop(0)
    fw_ref = refs.pop(0) if with_final else None
    out_ref = refs.pop(0)

    x = x_ref[0]
    if with_proj:
        x = x + mod_ref[0, 0, 2:3, :] * _dot_tn(o_ref[0], wo_ref[0])
    shift, scale, gate = mod_ref[0, 0, 3:4, :], mod_ref[0, 0, 4:5, :], mod_ref[0, 0, 5:6, :]
    h = _rms_mod(x, nw_ref[...], shift, scale).astype(BF16)
    g = _dot(h, wg_ref[0])
    u = _dot(h, wu_ref[0])
    act = (g * jax.nn.sigmoid(g) * u).astype(BF16)
    x = x + gate * _dot(act, wd_ref[0])
    if with_final:
        x = _rms(x, fw_ref[...])
    out_ref[0] = x


def _ffn(x, o_t, mods, layer, norm_w, w_o, w_gate, w_up, w_down, final_w, tm):
    b, n, d = x.shape
    with_proj = o_t is not None
    with_final = final_w is not None
    tile = pl.BlockSpec((1, tm, d), lambda bi, i: (bi, i, 0))
    args, specs = [x], [tile]
    if with_proj:
        args.append(o_t)
        specs.append(pl.BlockSpec((1, d, tm), lambda bi, i: (bi, 0, i)))
    args += [mods, norm_w]
    specs += [_mod_spec(layer, lambda bi, i: bi), _resident((1, d))]
    if with_proj:
        args.append(w_o)
        specs.append(_resident_layer(w_o.shape, 0))
    args += [w_gate, w_up, w_down]
    specs += [_resident_layer(w.shape, layer) for w in (w_gate, w_up, w_down)]
    if with_final:
        args.append(final_w)
        specs.append(_resident((1, d)))
    return pl.pallas_call(
        functools.partial(_ffn_kernel, with_proj=with_proj, with_final=with_final),
        grid=(b, n // tm),
        in_specs=specs,
        out_specs=tile,
        out_shape=jax.ShapeDtypeStruct((b, n, d), F32),
        compiler_params=_params("arbitrary", "arbitrary"),
        name="attn_proj_ffn" if with_proj else "ffn_final",
    )(*args)


def _conv_kernel(x_ref, xp_ref, xn_ref, mod_ref, nw_ref, win_ref, cw_ref, wout_ref, out_ref, z_ref):
    i = pl.program_id(1)
    last = pl.num_programs(1) - 1
    d = x_ref.shape[2]
    tm = x_ref.shape[1]
    x = x_ref[0]
    xe = jnp.concatenate([xp_ref[0], x, xn_ref[0]], axis=0)
    shift, scale, gate = mod_ref[0, 0, 0:1, :], mod_ref[0, 0, 1:2, :], mod_ref[0, 0, 2:3, :]
    h = _rms_mod(xe, nw_ref[...], shift, scale).astype(BF16)
    p = _dot(h, win_ref[0])
    z = p[:, d:2 * d] * p[:, 2 * d:3 * d]
    row = lax.broadcasted_iota(jnp.int32, (tm + 2 * HALO, 1), 0)
    outside = ((row < HALO) & (i == 0)) | ((row >= tm + HALO) & (i == last))
    z_ref[...] = jnp.where(outside, 0.0, z)
    u = (cw_ref[0:1, :] * z_ref[HALO - 1:HALO - 1 + tm, :]
         + cw_ref[1:2, :] * z_ref[HALO:HALO + tm, :]
         + cw_ref[2:3, :] * z_ref[HALO + 1:HALO + 1 + tm, :])
    y = _dot((p[HALO:HALO + tm, 0:d] * u).astype(BF16), wout_ref[0])
    out_ref[0] = x + gate * y


def _short_conv(x, mods, layer, norm_w, w_in, conv_w, w_out, tm):
    b, n, d = x.shape
    nblk = n // HALO
    per = tm // HALO
    return pl.pallas_call(
        _conv_kernel,
        grid=(b, n // tm),
        in_specs=[
            pl.BlockSpec((1, tm, d), lambda bi, i: (bi, i, 0)),
            pl.BlockSpec((1, HALO, d), lambda bi, i: (bi, jnp.maximum(i * per - 1, 0), 0)),
            pl.BlockSpec((1, HALO, d), lambda bi, i: (bi, jnp.minimum((i + 1) * per, nblk - 1), 0)),
            _mod_spec(layer, lambda bi, i: bi),
            _resident((1, d)),
            _resident_layer(w_in.shape, 0),
            _resident(conv_w.shape),
            _resident_layer(w_out.shape, 0),
        ],
        out_specs=pl.BlockSpec((1, tm, d), lambda bi, i: (bi, i, 0)),
        out_shape=jax.ShapeDtypeStruct((b, n, d), F32),
        scratch_shapes=[pltpu.VMEM((tm + 2 * HALO, d), F32)],
        compiler_params=_params("arbitrary", "arbitrary"),
        name="short_conv",
    )(x, x, x, mods, norm_w, w_in, conv_w, w_out)


def _rope_tables(n):
    rows = n // GRID_W
    n_freq = QK_ROPE // 4
    freqs = ROPE_THETA ** (-jnp.arange(n_freq, dtype=F32) / n_freq)
    ar = jnp.arange(rows).astype(F32)[:, None] * freqs
    ac = jnp.arange(GRID_W).astype(F32)[:, None] * freqs

    def table(fn):
        r = jnp.broadcast_to(fn(ar)[:, None, :], (rows, GRID_W, n_freq)).reshape(n, n_freq)
        c = jnp.broadcast_to(fn(ac)[None, :, :], (rows, GRID_W, n_freq)).reshape(n, n_freq)
        return jnp.concatenate([r, r, c, c], axis=-1)

    return table(jnp.cos), table(jnp.sin)


def _rotate_cols(w):
    q = QK_ROPE // 4
    perm = np.concatenate([np.arange(q, 2 * q), np.arange(0, q), np.arange(3 * q, 4 * q), np.arange(2 * q, 3 * q)])
    sign = np.concatenate([-np.ones(q), np.ones(q), -np.ones(q), np.ones(q)]).astype(np.float32)
    return w[..., perm] * sign


def kernel(x, c, ctx, c_ctx, ada_w, ada_b, norm_mix, norm_ffn, mla_w_dq, mla_q_norm, mla_w_uq, mla_w_dkv,
           mla_kv_norm, mla_w_ukv, mla_w_o, conv_w_in, conv_w, conv_w_out, ffn_w_gate, ffn_w_up, ffn_w_down,
           final_norm):
    b, n, d = x.shape

    pad = (-(b + 1)) % 8
    c_rows = jnp.concatenate([c, c_ctx[None, :], jnp.zeros((pad, d), F32)], axis=0)
    mods = _modulation(c_rows, ada_w, ada_b)
    mods = mods.reshape(mods.shape[0], mods.shape[1], 6, d)

    w_uq = mla_w_uq[0].reshape(Q_LORA, N_HEADS, QK_DIM)
    uq_rope = w_uq[:, :, QK_NOPE:]
    w_uq_t = jnp.concatenate([w_uq[:, :, :QK_NOPE].reshape(Q_LORA, -1), uq_rope.reshape(Q_LORA, -1),
                              _rotate_cols(uq_rope).reshape(Q_LORA, -1)], axis=1).T.astype(BF16)
    w_dkv = mla_w_dkv[0]
    w_down = jnp.concatenate([mla_w_dq[0], w_dkv, _rotate_cols(w_dkv[:, KV_LORA:])], axis=1).astype(BF16)
    w_ukv = mla_w_ukv[0].reshape(KV_LORA, N_HEADS, QK_NOPE + V_HEAD)
    w_uk = w_ukv[:, :, :QK_NOPE].reshape(KV_LORA, -1).astype(BF16)
    w_v_t = w_ukv[:, :, QK_NOPE:].reshape(KV_LORA, -1).T.astype(BF16)
    tc, ts = _rope_tables(n)
    tct, tst = tc.T, ts.T

    q_t, k, v_t, k_ctx, v_ctx_t = _mla_qkv(x, ctx, mods, norm_mix[0][None], mla_q_norm[0][None],
                                           mla_kv_norm[0][None], tc, ts, tct, tst, w_down, w_uq_t, w_uk, w_v_t,
                                           tq=min(ATTN_TQ, n), tk=min(ATTN_TK, n))
    o_t = _attention(q_t, k, v_t, k_ctx, v_ctx_t)
    w_gate, w_up, w_dn = ffn_w_gate.astype(BF16), ffn_w_up.astype(BF16), ffn_w_down.astype(BF16)
    x = _ffn(x, o_t, mods, 0, norm_ffn[0][None], mla_w_o.astype(BF16), w_gate, w_up, w_dn, None, tm=min(512, n))

    x = _short_conv(x, mods, 1, norm_mix[1][None], conv_w_in.astype(BF16), conv_w[0], conv_w_out.astype(BF16),
                    tm=min(512, n))
    return _ffn(x, None, mods, 1, norm_ffn[1][None], None, w_gate, w_up, w_dn, final_norm[None], tm=min(512, n))
```

```python
import functools
import math

import numpy as np
import jax
import jax.numpy as jnp
from jax import lax
from jax.experimental import pallas as pl
from jax.experimental.pallas import tpu as pltpu

D_MODEL = 1024
GRID_W = 64
N_HEADS = 8
QK_NOPE = 128
QK_ROPE = 64
V_HEAD = 128
Q_LORA = 256
KV_LORA = 128
QK_DIM = QK_NOPE + QK_ROPE
V_ROWS = V_HEAD + 16
ROPE_THETA = 10000.0
NORM_EPS = 1e-6
ATTN_SCALE = 1.0 / math.sqrt(QK_NOPE + QK_ROPE)
LOG2_E = 1.4426950408889634
LAG_LIMIT = 60.0
ATTN_TQ = 1024
ATTN_TK = 512
FF_CHUNK = 256

F32 = jnp.float32
BF16 = jnp.bfloat16

VMEM_LIMIT_BYTES = 60 * 1024 * 1024
HALO = 8


def _params(*sem):
    return pltpu.CompilerParams(dimension_semantics=sem, vmem_limit_bytes=VMEM_LIMIT_BYTES)


def _resident(shape):
    zeros = (0,) * len(shape)
    return pl.BlockSpec(shape, lambda *_: zeros, pipeline_mode=pl.Buffered(1))


def _resident_layer(shape, layer):
    rest = (0,) * (len(shape) - 1)
    return pl.BlockSpec((1,) + tuple(shape[1:]), lambda *_: (layer,) + rest, pipeline_mode=pl.Buffered(1))


def _mod_spec(layer, row_of):
    return pl.BlockSpec((1, 1, 6, D_MODEL), lambda *idx: (layer, row_of(*idx), 0, 0))


def _rms(x, w):
    return x * lax.rsqrt(jnp.mean(x * x, axis=-1, keepdims=True) + NORM_EPS) * w


def _rms_mod(x, w, shift, scale):
    return x * lax.rsqrt(jnp.mean(x * x, axis=-1, keepdims=True) + NORM_EPS) * (w * (1.0 + scale)) + shift


def _dot(a, b):
    return jnp.dot(a, b, preferred_element_type=F32)


def _dot_tn(a, b):
    return lax.dot_general(a, b, (((0,), (0,)), ((), ())), preferred_element_type=F32)


def _dot_nt(a, b):
    return lax.dot_general(a, b, (((1,), (1,)), ((), ())), preferred_element_type=F32)


def _mod_kernel(c_ref, w_ref, b_ref, o_ref):
    c = c_ref[...]
    o_ref[0] = _dot(c * jax.nn.sigmoid(c), w_ref[0]) + b_ref[0]


def _modulation(c_rows, ada_w, ada_b):
    depth, d, d6 = ada_w.shape
    rows = c_rows.shape[0]
    tn = 1536
    return pl.pallas_call(
        _mod_kernel,
        grid=(depth, d6 // tn),
        in_specs=[
            pl.BlockSpec((rows, d), lambda i, n: (0, 0)),
            pl.BlockSpec((1, d, tn), lambda i, n: (i, 0, n)),
            pl.BlockSpec((1, 1, tn), lambda i, n: (i, 0, n)),
        ],
        out_specs=pl.BlockSpec((1, rows, tn), lambda i, n: (i, 0, n)),
        out_shape=jax.ShapeDtypeStruct((depth, rows, d6), F32),
        compiler_params=_params("arbitrary", "arbitrary"),
        name="adaln_modulation",
    )(c_rows, ada_w, ada_b.reshape(depth, 1, d6))


def _kv_store(ckv, k_rope, wuk_ref, wvt_ref, k_ref, v_ref):
    k_nope = _dot(ckv, wuk_ref[...])
    v_t = _dot_nt(wvt_ref[...], ckv)
    tk = v_ref.shape[4]
    ones = jnp.ones((V_ROWS - V_HEAD, tk), BF16)
    for hd in range(N_HEADS):
        k_ref[0, hd, :, 0:QK_NOPE] = k_nope[:, hd * QK_NOPE:(hd + 1) * QK_NOPE].astype(BF16)
        k_ref[0, hd, :, QK_NOPE:QK_DIM] = k_rope
        for cc in range(v_ref.shape[2]):
            v_ref[0, hd, cc, 0:V_HEAD, :] = v_t[hd * V_HEAD:(hd + 1) * V_HEAD, cc * tk:(cc + 1) * tk].astype(BF16)
            v_ref[0, hd, cc, V_HEAD:V_ROWS, :] = ones


def _qkv_latent_kernel(x_ref, mod_ref, nw_ref, qn_ref, kvn_ref, tc_ref, ts_ref, tct_ref, tst_ref,
                       wdn_ref, wuqt_ref, wuk_ref, wvt_ref, q_ref, k_ref, v_ref):
    shift, scale = mod_ref[0, 0, 0:1, :], mod_ref[0, 0, 1:2, :]
    h = _rms_mod(x_ref[0], nw_ref[...], shift, scale).astype(BF16)
    down = _dot(h, wdn_ref[...])
    ckv = _rms(down[:, Q_LORA:Q_LORA + KV_LORA], kvn_ref[...]).astype(BF16)
    kr0 = Q_LORA + KV_LORA
    k_rope = (down[:, kr0:kr0 + QK_ROPE] * tc_ref[...]
              + down[:, kr0 + QK_ROPE:kr0 + 2 * QK_ROPE] * ts_ref[...]).astype(BF16)
    _kv_store(ckv, k_rope, wuk_ref, wvt_ref, k_ref, v_ref)

    cq = _rms(down[:, 0:Q_LORA], qn_ref[...] * (ATTN_SCALE * LOG2_E)).astype(BF16)
    q_t = _dot_nt(wuqt_ref[...], cq)
    r0 = N_HEADS * QK_NOPE
    r1 = r0 + N_HEADS * QK_ROPE
    cos2 = jnp.concatenate([tct_ref[...], tct_ref[...]], axis=0)
    sin2 = jnp.concatenate([tst_ref[...], tst_ref[...]], axis=0)
    for pair in range(N_HEADS // 2):
        lo = 2 * QK_ROPE * pair
        rope2 = (q_t[r0 + lo:r0 + lo + 2 * QK_ROPE, :] * cos2
                 + q_t[r1 + lo:r1 + lo + 2 * QK_ROPE, :] * sin2).astype(BF16)
        for sub in range(2):
            hd = 2 * pair + sub
            q_ref[0, hd, 0, 0:QK_NOPE, :] = q_t[hd * QK_NOPE:(hd + 1) * QK_NOPE, :].astype(BF16)
            q_ref[0, hd, 0, QK_NOPE:QK_DIM, :] = rope2[sub * QK_ROPE:(sub + 1) * QK_ROPE, :]


def _kv_context_kernel(x_ref, mod_ref, nw_ref, kvn_ref, wdn_ref, wuk_ref, wvt_ref, k_ref, v_ref):
    shift, scale = mod_ref[0, 0, 0:1, :], mod_ref[0, 0, 1:2, :]
    h = _rms_mod(x_ref[0], nw_ref[...], shift, scale).astype(BF16)
    down = _dot(h, wdn_ref[...])
    ckv = _rms(down[:, Q_LORA:Q_LORA + KV_LORA], kvn_ref[...]).astype(BF16)
    kr0 = Q_LORA + KV_LORA
    k_rope = down[:, kr0:kr0 + QK_ROPE].astype(BF16)
    _kv_store(ckv, k_rope, wuk_ref, wvt_ref, k_ref, v_ref)


def _mla_qkv(x, ctx, mods, norm_w, q_norm, kv_norm, tc, ts, tct, tst, w_down, w_uq_t, w_uk, w_v_t, tq, tk):
    b, n, d = x.shape
    n_ctx = ctx.shape[1]
    tm = tq
    assert n % tm == 0 and tm % tk == 0
    q_t, k, v_t = pl.pallas_call(
        _qkv_latent_kernel,
        grid=(n // tm, b),
        in_specs=[
            pl.BlockSpec((1, tm, d), lambda j, bi: (bi, j, 0)),
            _mod_spec(0, lambda j, bi: bi),
            _resident((1, d)),
            _resident((1, Q_LORA)),
            _resident((1, KV_LORA)),
            pl.BlockSpec((tm, QK_ROPE), lambda j, bi: (j, 0)),
            pl.BlockSpec((tm, QK_ROPE), lambda j, bi: (j, 0)),
            pl.BlockSpec((QK_ROPE, tm), lambda j, bi: (0, j)),
            pl.BlockSpec((QK_ROPE, tm), lambda j, bi: (0, j)),
            _resident(w_down.shape),
            _resident(w_uq_t.shape),
            _resident(w_uk.shape),
            _resident(w_v_t.shape),
        ],
        out_specs=[
            pl.BlockSpec((1, N_HEADS, 1, QK_DIM, tm), lambda j, bi: (bi, 0, j, 0, 0)),
            pl.BlockSpec((1, N_HEADS, tm, QK_DIM), lambda j, bi: (bi, 0, j, 0)),
            pl.BlockSpec((1, N_HEADS, tm // tk, V_ROWS, tk), lambda j, bi: (bi, 0, j, 0, 0)),
        ],
        out_shape=[jax.ShapeDtypeStruct((b, N_HEADS, n // tq, QK_DIM, tq), BF16),
                   jax.ShapeDtypeStruct((b, N_HEADS, n, QK_DIM), BF16),
                   jax.ShapeDtypeStruct((b, N_HEADS, n // tk, V_ROWS, tk), BF16)],
        compiler_params=_params("arbitrary", "arbitrary"),
        name="mla_qkv_latent",
    )(x, mods, norm_w, q_norm, kv_norm, tc, ts, tct, tst, w_down, w_uq_t, w_uk, w_v_t)
    k_ctx, v_ctx_t = pl.pallas_call(
        _kv_context_kernel,
        grid=(b,),
        in_specs=[
            pl.BlockSpec((1, n_ctx, d), lambda bi: (bi, 0, 0)),
            _mod_spec(0, lambda bi: b),
            _resident((1, d)),
            _resident((1, KV_LORA)),
            _resident(w_down.shape),
            _resident(w_uk.shape),
            _resident(w_v_t.shape),
        ],
        out_specs=[
            pl.BlockSpec((1, N_HEADS, n_ctx, QK_DIM), lambda bi: (bi, 0, 0, 0)),
            pl.BlockSpec((1, N_HEADS, 1, V_ROWS, n_ctx), lambda bi: (bi, 0, 0, 0, 0)),
        ],
        out_shape=[jax.ShapeDtypeStruct((b, N_HEADS, n_ctx, QK_DIM), BF16),
                   jax.ShapeDtypeStruct((b, N_HEADS, 1, V_ROWS, n_ctx), BF16)],
        compiler_params=_params("arbitrary"),
        name="mla_kv_context",
    )(ctx, mods, norm_w, kv_norm, w_down, w_uk, w_v_t)
    return q_t, k, v_t, k_ctx, v_ctx_t


def _attn_kernel(q_ref, k_ref, v_ref, kc_ref, vc_ref, o_ref, *, tk):
    q_t = q_ref[0, 0, 0]
    tq = q_t.shape[1]
    n_chunks = v_ref.shape[2]

    def scores(c):
        return _dot(k_ref[0, 0, c * tk:(c + 1) * tk, :], q_t)

    def pv(c, p):
        return _dot(v_ref[0, 0, c], p)

    def finish(acc):
        o_ref[0] = (acc[0:V_HEAD, :] / acc[V_HEAD:V_HEAD + 1, :]).astype(BF16)

    def context_softmax():
        s = _dot(kc_ref[0, 0], q_t)
        ref = jnp.max(s, axis=0, keepdims=True)
        return ref, jnp.exp2(s - ref).astype(BF16)

    ref, p_ctx = context_softmax()
    ahead = [scores(c) for c in range(min(2, n_chunks))]
    acc = _dot(vc_ref[0, 0, 0], p_ctx)
    acc_ref = ref
    over = jnp.zeros((1, tq), F32)
    for c in range(n_chunks):
        s = ahead.pop(0)
        if c + 2 < n_chunks:
            ahead.append(scores(c + 2))
        chunk_max = jnp.max(s, axis=0, keepdims=True)
        p = jnp.exp2(s - ref).astype(BF16)
        acc = jnp.exp2(acc_ref - ref) * acc + pv(c, p)
        acc_ref = ref
        over = jnp.maximum(over, chunk_max - ref)
        ref = jnp.maximum(ref, chunk_max)
    finish(acc)

    @pl.when(jnp.max(over) > LAG_LIMIT)
    def _():
        def body(c, carry):
            m, acc = carry
            k0 = pl.multiple_of(c * tk, tk)
            s = _dot(k_ref[0, 0, pl.ds(k0, tk), :], q_t)
            m_new = jnp.maximum(m, jnp.max(s, axis=0, keepdims=True))
            p = jnp.exp2(s - m_new).astype(BF16)
            acc = jnp.exp2(m - m_new) * acc + _dot(v_ref[0, 0, c], p)
            return m_new, acc

        m0, p0 = context_softmax()
        finish(lax.fori_loop(0, n_chunks, body, (m0, _dot(vc_ref[0, 0, 0], p0)))[1])


def _attention(q_t, k, v_t, k_ctx, v_ctx_t):
    b, nh, n_tiles, _, tq = q_t.shape
    n = n_tiles * tq
    n_chunks, tk = v_t.shape[2], v_t.shape[4]
    n_ctx = k_ctx.shape[2]
    head = lambda bi, hi, i: (bi, hi, 0, 0)
    head5 = lambda bi, hi, i: (bi, hi, 0, 0, 0)
    return pl.pallas_call(
        functools.partial(_attn_kernel, tk=tk),
        grid=(b, nh, n_tiles),
        in_specs=[
            pl.BlockSpec((1, 1, 1, QK_DIM, tq), lambda bi, hi, i: (bi, hi, i, 0, 0)),
            pl.BlockSpec((1, 1, n, QK_DIM), head),
            pl.BlockSpec((1, 1, n_chunks, V_ROWS, tk), head5),
            pl.BlockSpec((1, 1, n_ctx, QK_DIM), head),
            pl.BlockSpec((1, 1, 1, V_ROWS, n_ctx), head5),
        ],
        out_specs=pl.BlockSpec((1, V_HEAD, tq), lambda bi, hi, i: (bi, hi, i)),
        out_shape=jax.ShapeDtypeStruct((b, nh * V_HEAD, n), BF16),
        compiler_params=_params("arbitrary", "arbitrary", "arbitrary"),
        name="mla_attention",
    )(q_t, k, v_t, k_ctx, v_ctx_t)


def _ffn_kernel(*refs, with_proj, with_final):
    refs = list(refs)
    x_ref = refs.pop(0)
    o_ref = refs.pop(0) if with_proj else None
    mod_ref = refs.pop(0)
    nw_ref = refs.pop(0)
    wo_ref = refs.pop(0) if with_proj else None
    wg_ref, wu_ref, wd_ref = refs.pop(0), refs.pop(0), refs.pop(0)
    fw_ref = refs.pop(0) if with_final else None
    out_ref = refs.pop(0)

    x = x_ref[0]
    if with_proj:
        x = x + mod_ref[0, 0, 2:3, :] * _dot_tn(o_ref[0], wo_ref[0])
    shift, scale, gate = mod_ref[0, 0, 3:4, :], mod_ref[0, 0, 4:5, :], mod_ref[0, 0, 5:6, :]
    h = _rms_mod(x, nw_ref[...], shift, scale).astype(BF16)
    y = None
    for c0 in range(0, wg_ref.shape[2], FF_CHUNK):
        g = _dot(h, wg_ref[0, :, c0:c0 + FF_CHUNK])
        u = _dot(h, wu_ref[0, :, c0:c0 + FF_CHUNK])
        act = (g * jax.nn.sigmoid(g) * u).astype(BF16)
        part = _dot(act, wd_ref[0, c0:c0 + FF_CHUNK, :])
        y = part if y is None else y + part
    x = x + gate * y
    if with_final:
        x = _rms(x, fw_ref[...])
    out_ref[0] = x


def _ffn(x, o_t, mods, layer, norm_w, w_o, w_gate, w_up, w_down, final_w, tm):
    b, n, d = x.shape
    with_proj = o_t is not None
    with_final = final_w is not None
    tile = pl.BlockSpec((1, tm, d), lambda bi, i: (bi, i, 0))
    args, specs = [x], [tile]
    if with_proj:
        args.append(o_t)
        specs.append(pl.BlockSpec((1, d, tm), lambda bi, i: (bi, 0, i)))
    args += [mods, norm_w]
    specs += [_mod_spec(layer, lambda bi, i: bi), _resident((1, d))]
    if with_proj:
        args.append(w_o)
        specs.append(_resident_layer(w_o.shape, 0))
    args += [w_gate, w_up, w_down]
    specs += [_resident_layer(w.shape, layer) for w in (w_gate, w_up, w_down)]
    if with_final:
        args.append(final_w)
        specs.append(_resident((1, d)))
    return pl.pallas_call(
        functools.partial(_ffn_kernel, with_proj=with_proj, with_final=with_final),
        grid=(b, n // tm),
        in_specs=specs,
        out_specs=tile,
        out_shape=jax.ShapeDtypeStruct((b, n, d), F32),
        compiler_params=_params("arbitrary", "arbitrary"),
        name="attn_proj_ffn" if with_proj else "ffn_final",
    )(*args)


def _conv_kernel(x_ref, xp_ref, xn_ref, mod_ref, nw_ref, win_ref, cw_ref, wout_ref, out_ref, z_ref):
    i = pl.program_id(1)
    last = pl.num_programs(1) - 1
    d = x_ref.shape[2]
    tm = x_ref.shape[1]
    x = x_ref[0]
    xe = jnp.concatenate([xp_ref[0], x, xn_ref[0]], axis=0)
    shift, scale, gate = mod_ref[0, 0, 0:1, :], mod_ref[0, 0, 1:2, :], mod_ref[0, 0, 2:3, :]
    h = _rms_mod(xe, nw_ref[...], shift, scale).astype(BF16)
    p = _dot(h, win_ref[0])
    z = p[:, d:2 * d] * p[:, 2 * d:3 * d]
    row = lax.broadcasted_iota(jnp.int32, (tm + 2 * HALO, 1), 0)
    outside = ((row < HALO) & (i == 0)) | ((row >= tm + HALO) & (i == last))
    z_ref[...] = jnp.where(outside, 0.0, z)
    u = (cw_ref[0:1, :] * z_ref[HALO - 1:HALO - 1 + tm, :]
         + cw_ref[1:2, :] * z_ref[HALO:HALO + tm, :]
         + cw_ref[2:3, :] * z_ref[HALO + 1:HALO + 1 + tm, :])
    y = _dot((p[HALO:HALO + tm, 0:d] * u).astype(BF16), wout_ref[0])
    out_ref[0] = x + gate * y


def _short_conv(x, mods, layer, norm_w, w_in, conv_w, w_out, tm):
    b, n, d = x.shape
    nblk = n // HALO
    per = tm // HALO
    return pl.pallas_call(
        _conv_kernel,
        grid=(b, n // tm),
        in_specs=[
            pl.BlockSpec((1, tm, d), lambda bi, i: (bi, i, 0)),
            pl.BlockSpec((1, HALO, d), lambda bi, i: (bi, jnp.maximum(i * per - 1, 0), 0)),
            pl.BlockSpec((1, HALO, d), lambda bi, i: (bi, jnp.minimum((i + 1) * per, nblk - 1), 0)),
            _mod_spec(layer, lambda bi, i: bi),
            _resident((1, d)),
            _resident_layer(w_in.shape, 0),
            _resident(conv_w.shape),
            _resident_layer(w_out.shape, 0),
        ],
        out_specs=pl.BlockSpec((1, tm, d), lambda bi, i: (bi, i, 0)),
        out_shape=jax.ShapeDtypeStruct((b, n, d), F32),
        scratch_shapes=[pltpu.VMEM((tm + 2 * HALO, d), F32)],
        compiler_params=_params("arbitrary", "arbitrary"),
        name="short_conv",
    )(x, x, x, mods, norm_w, w_in, conv_w, w_out)


def _rope_tables(n):
    rows = n // GRID_W
    n_freq = QK_ROPE // 4
    freqs = ROPE_THETA ** (-jnp.arange(n_freq, dtype=F32) / n_freq)
    ar = jnp.arange(rows).astype(F32)[:, None] * freqs
    ac = jnp.arange(GRID_W).astype(F32)[:, None] * freqs

    def table(fn):
        r = jnp.broadcast_to(fn(ar)[:, None, :], (rows, GRID_W, n_freq)).reshape(n, n_freq)
        c = jnp.broadcast_to(fn(ac)[None, :, :], (rows, GRID_W, n_freq)).reshape(n, n_freq)
        return jnp.concatenate([r, r, c, c], axis=-1)

    return table(jnp.cos), table(jnp.sin)


def _rotate_cols(w):
    q = QK_ROPE // 4
    perm = np.concatenate([np.arange(q, 2 * q), np.arange(0, q), np.arange(3 * q, 4 * q), np.arange(2 * q, 3 * q)])
    sign = np.concatenate([-np.ones(q), np.ones(q), -np.ones(q), np.ones(q)]).astype(np.float32)
    return w[..., perm] * sign


def kernel(x, c, ctx, c_ctx, ada_w, ada_b, norm_mix, norm_ffn, mla_w_dq, mla_q_norm, mla_w_uq, mla_w_dkv,
           mla_kv_norm, mla_w_ukv, mla_w_o, conv_w_in, conv_w, conv_w_out, ffn_w_gate, ffn_w_up, ffn_w_down,
           final_norm):
    b, n, d = x.shape

    pad = (-(b + 1)) % 8
    c_rows = jnp.concatenate([c, c_ctx[None, :], jnp.zeros((pad, d), F32)], axis=0)
    mods = _modulation(c_rows, ada_w, ada_b)
    mods = mods.reshape(mods.shape[0], mods.shape[1], 6, d)

    w_uq = mla_w_uq[0].reshape(Q_LORA, N_HEADS, QK_DIM)
    uq_rope = w_uq[:, :, QK_NOPE:]
    w_uq_t = jnp.concatenate([w_uq[:, :, :QK_NOPE].reshape(Q_LORA, -1), uq_rope.reshape(Q_LORA, -1),
                              _rotate_cols(uq_rope).reshape(Q_LORA, -1)], axis=1).T.astype(BF16)
    w_dkv = mla_w_dkv[0]
    w_down = jnp.concatenate([mla_w_dq[0], w_dkv, _rotate_cols(w_dkv[:, KV_LORA:])], axis=1).astype(BF16)
    w_ukv = mla_w_ukv[0].reshape(KV_LORA, N_HEADS, QK_NOPE + V_HEAD)
    w_uk = w_ukv[:, :, :QK_NOPE].reshape(KV_LORA, -1).astype(BF16)
    w_v_t = w_ukv[:, :, QK_NOPE:].reshape(KV_LORA, -1).T.astype(BF16)
    tc, ts = _rope_tables(n)
    tct, tst = tc.T, ts.T

    q_t, k, v_t, k_ctx, v_ctx_t = _mla_qkv(x, ctx, mods, norm_mix[0][None], mla_q_norm[0][None],
                                           mla_kv_norm[0][None], tc, ts, tct, tst, w_down, w_uq_t, w_uk, w_v_t,
                                           tq=min(ATTN_TQ, n), tk=min(ATTN_TK, n))
    o_t = _attention(q_t, k, v_t, k_ctx, v_ctx_t)
    w_gate, w_up, w_dn = ffn_w_gate.astype(BF16), ffn_w_up.astype(BF16), ffn_w_down.astype(BF16)
    x = _ffn(x, o_t, mods, 0, norm_ffn[0][None], mla_w_o.astype(BF16), w_gate, w_up, w_dn, None, tm=min(1024, n))

    x = _short_conv(x, mods, 1, norm_mix[1][None], conv_w_in.astype(BF16), conv_w[0], conv_w_out.astype(BF16),
                    tm=min(1024, n))
    return _ffn(x, None, mods, 1, norm_ffn[1][None], None, w_gate, w_up, w_dn, final_norm[None], tm=min(1024, n))
```

```python
import functools
import math

import numpy as np
import jax
import jax.numpy as jnp
from jax import lax
from jax.experimental import pallas as pl
from jax.experimental.pallas import tpu as pltpu

D_MODEL = 1024
GRID_W = 64
N_HEADS = 8
QK_NOPE = 128
QK_ROPE = 64
V_HEAD = 128
Q_LORA = 256
KV_LORA = 128
QK_DIM = QK_NOPE + QK_ROPE
V_ROWS = V_HEAD + 16
ROPE_THETA = 10000.0
NORM_EPS = 1e-6
ATTN_SCALE = 1.0 / math.sqrt(QK_NOPE + QK_ROPE)
LOG2_E = 1.4426950408889634
LAG_LIMIT = 60.0
ATTN_TQ = 1024
ATTN_TK = 512
FF_CHUNK = 256

F32 = jnp.float32
BF16 = jnp.bfloat16

VMEM_LIMIT_BYTES = 60 * 1024 * 1024
HALO = 8


def _params(*sem):
    return pltpu.CompilerParams(dimension_semantics=sem, vmem_limit_bytes=VMEM_LIMIT_BYTES)


def _resident(shape):
    zeros = (0,) * len(shape)
    return pl.BlockSpec(shape, lambda *_: zeros, pipeline_mode=pl.Buffered(1))


def _resident_layer(shape, layer):
    rest = (0,) * (len(shape) - 1)
    return pl.BlockSpec((1,) + tuple(shape[1:]), lambda *_: (layer,) + rest, pipeline_mode=pl.Buffered(1))


def _mod_spec(layer, row_of):
    return pl.BlockSpec((1, 1, 6, D_MODEL), lambda *idx: (layer, row_of(*idx), 0, 0))


def _rms(x, w):
    return x * lax.rsqrt(jnp.mean(x * x, axis=-1, keepdims=True) + NORM_EPS) * w


def _rms_mod(x, w, shift, scale):
    return x * lax.rsqrt(jnp.mean(x * x, axis=-1, keepdims=True) + NORM_EPS) * (w * (1.0 + scale)) + shift


def _dot(a, b):
    return jnp.dot(a, b, preferred_element_type=F32)


def _dot_tn(a, b):
    return lax.dot_general(a, b, (((0,), (0,)), ((), ())), preferred_element_type=F32)


def _dot_nt(a, b):
    return lax.dot_general(a, b, (((1,), (1,)), ((), ())), preferred_element_type=F32)


def _mod_kernel(c_ref, w_ref, b_ref, o_ref):
    c = c_ref[...]
    o_ref[0] = _dot(c * jax.nn.sigmoid(c), w_ref[0]) + b_ref[0]


def _modulation(c_rows, ada_w, ada_b):
    depth, d, d6 = ada_w.shape
    rows = c_rows.shape[0]
    tn = 1536
    return pl.pallas_call(
        _mod_kernel,
        grid=(depth, d6 // tn),
        in_specs=[
            pl.BlockSpec((rows, d), lambda i, n: (0, 0)),
            pl.BlockSpec((1, d, tn), lambda i, n: (i, 0, n)),
            pl.BlockSpec((1, 1, tn), lambda i, n: (i, 0, n)),
        ],
        out_specs=pl.BlockSpec((1, rows, tn), lambda i, n: (i, 0, n)),
        out_shape=jax.ShapeDtypeStruct((depth, rows, d6), F32),
        compiler_params=_params("arbitrary", "arbitrary"),
        name="adaln_modulation",
    )(c_rows, ada_w, ada_b.reshape(depth, 1, d6))


def _fold_kernel(wqn_ref, wuk_ref, wuv_ref, wo_ref, qabs_ref, ofold_ref):
    hp = lax.Precision.HIGHEST
    for hd in range(wqn_ref.shape[0]):
        qabs_ref[hd] = lax.dot_general(wuk_ref[hd], wqn_ref[hd], (((1,), (1,)), ((), ())), precision=hp,
                                       preferred_element_type=F32)
        ofold_ref[hd] = jnp.dot(wuv_ref[hd], wo_ref[hd], precision=hp, preferred_element_type=F32)


def _fold_weights(w_qnope, w_uk, w_uv, w_o):
    nh, q_lora, _ = w_qnope.shape
    d = w_o.shape[2]
    group = nh // 2
    per_head = lambda shape: pl.BlockSpec((group,) + shape, lambda h: (h, 0, 0))
    return pl.pallas_call(
        _fold_kernel,
        grid=(nh // group,),
        in_specs=[per_head((q_lora, QK_NOPE)), per_head((KV_LORA, QK_NOPE)), per_head((KV_LORA, V_HEAD)),
                  per_head((V_HEAD, d))],
        out_specs=[per_head((KV_LORA, q_lora)), per_head((KV_LORA, d))],
        out_shape=[jax.ShapeDtypeStruct((nh, KV_LORA, q_lora), F32), jax.ShapeDtypeStruct((nh, KV_LORA, d), F32)],
        compiler_params=_params("arbitrary"),
        name="mla_fold_weights",
    )(w_qnope, w_uk, w_uv, w_o)


def _kv_store(ckv, k_rope, k_ref, v_ref):
    k_ref[0, :, 0:KV_LORA] = ckv.astype(BF16)
    k_ref[0, :, KV_LORA:QK_DIM] = k_rope
    ckv_t = ckv.T.astype(BF16)
    tk = v_ref.shape[3]
    ones = jnp.ones((V_ROWS - V_HEAD, tk), BF16)
    for cc in range(v_ref.shape[1]):
        v_ref[0, cc, 0:V_HEAD, :] = ckv_t[:, cc * tk:(cc + 1) * tk]
        v_ref[0, cc, V_HEAD:V_ROWS, :] = ones


def _qkv_latent_kernel(x_ref, mod_ref, nw_ref, qn_ref, kvn_ref, tc_ref, ts_ref, tct_ref, tst_ref,
                       wdn_ref, wuqt_ref, q_ref, k_ref, v_ref):
    shift, scale = mod_ref[0, 0, 0:1, :], mod_ref[0, 0, 1:2, :]
    h = _rms_mod(x_ref[0], nw_ref[...], shift, scale).astype(BF16)
    down = _dot(h, wdn_ref[...])
    ckv = _rms(down[:, Q_LORA:Q_LORA + KV_LORA], kvn_ref[...])
    kr0 = Q_LORA + KV_LORA
    k_rope = (down[:, kr0:kr0 + QK_ROPE] * tc_ref[...]
              + down[:, kr0 + QK_ROPE:kr0 + 2 * QK_ROPE] * ts_ref[...]).astype(BF16)
    _kv_store(ckv, k_rope, k_ref, v_ref)

    cq = _rms(down[:, 0:Q_LORA], qn_ref[...] * (ATTN_SCALE * LOG2_E)).astype(BF16)
    q_t = _dot_nt(wuqt_ref[...], cq)
    r0 = N_HEADS * QK_NOPE
    r1 = r0 + N_HEADS * QK_ROPE
    cos2 = jnp.concatenate([tct_ref[...], tct_ref[...]], axis=0)
    sin2 = jnp.concatenate([tst_ref[...], tst_ref[...]], axis=0)
    for pair in range(N_HEADS // 2):
        lo = 2 * QK_ROPE * pair
        rope2 = (q_t[r0 + lo:r0 + lo + 2 * QK_ROPE, :] * cos2
                 + q_t[r1 + lo:r1 + lo + 2 * QK_ROPE, :] * sin2).astype(BF16)
        for sub in range(2):
            hd = 2 * pair + sub
            q_ref[0, hd, 0, 0:QK_NOPE, :] = q_t[hd * QK_NOPE:(hd + 1) * QK_NOPE, :].astype(BF16)
            q_ref[0, hd, 0, QK_NOPE:QK_DIM, :] = rope2[sub * QK_ROPE:(sub + 1) * QK_ROPE, :]


def _kv_context_kernel(x_ref, mod_ref, nw_ref, kvn_ref, wdn_ref, k_ref, v_ref):
    shift, scale = mod_ref[0, 0, 0:1, :], mod_ref[0, 0, 1:2, :]
    h = _rms_mod(x_ref[0], nw_ref[...], shift, scale).astype(BF16)
    down = _dot(h, wdn_ref[...])
    ckv = _rms(down[:, Q_LORA:Q_LORA + KV_LORA], kvn_ref[...])
    kr0 = Q_LORA + KV_LORA
    k_rope = down[:, kr0:kr0 + QK_ROPE].astype(BF16)
    _kv_store(ckv, k_rope, k_ref, v_ref)


def _mla_qkv(x, ctx, mods, norm_w, q_norm, kv_norm, tc, ts, tct, tst, w_down, w_uq_t, tq, tk):
    b, n, d = x.shape
    n_ctx = ctx.shape[1]
    tm = tq
    assert n % tm == 0 and tm % tk == 0
    q_t, k, v_t = pl.pallas_call(
        _qkv_latent_kernel,
        grid=(n // tm, b),
        in_specs=[
            pl.BlockSpec((1, tm, d), lambda j, bi: (bi, j, 0)),
            _mod_spec(0, lambda j, bi: bi),
            _resident((1, d)),
            _resident((1, Q_LORA)),
            _resident((1, KV_LORA)),
            pl.BlockSpec((tm, QK_ROPE), lambda j, bi: (j, 0)),
            pl.BlockSpec((tm, QK_ROPE), lambda j, bi: (j, 0)),
            pl.BlockSpec((QK_ROPE, tm), lambda j, bi: (0, j)),
            pl.BlockSpec((QK_ROPE, tm), lambda j, bi: (0, j)),
            _resident(w_down.shape),
            _resident(w_uq_t.shape),
        ],
        out_specs=[
            pl.BlockSpec((1, N_HEADS, 1, QK_DIM, tm), lambda j, bi: (bi, 0, j, 0, 0)),
            pl.BlockSpec((1, tm, QK_DIM), lambda j, bi: (bi, j, 0)),
            pl.BlockSpec((1, tm // tk, V_ROWS, tk), lambda j, bi: (bi, j, 0, 0)),
        ],
        out_shape=[jax.ShapeDtypeStruct((b, N_HEADS, n // tq, QK_DIM, tq), BF16),
                   jax.ShapeDtypeStruct((b, n, QK_DIM), BF16),
                   jax.ShapeDtypeStruct((b, n // tk, V_ROWS, tk), BF16)],
        compiler_params=_params("arbitrary", "arbitrary"),
        name="mla_qkv_latent",
    )(x, mods, norm_w, q_norm, kv_norm, tc, ts, tct, tst, w_down, w_uq_t)
    k_ctx, v_ctx_t = pl.pallas_call(
        _kv_context_kernel,
        grid=(b,),
        in_specs=[
            pl.BlockSpec((1, n_ctx, d), lambda bi: (bi, 0, 0)),
            _mod_spec(0, lambda bi: b),
            _resident((1, d)),
            _resident((1, KV_LORA)),
            _resident(w_down.shape),
        ],
        out_specs=[
            pl.BlockSpec((1, n_ctx, QK_DIM), lambda bi: (bi, 0, 0)),
            pl.BlockSpec((1, 1, V_ROWS, n_ctx), lambda bi: (bi, 0, 0, 0)),
        ],
        out_shape=[jax.ShapeDtypeStruct((b, n_ctx, QK_DIM), BF16),
                   jax.ShapeDtypeStruct((b, 1, V_ROWS, n_ctx), BF16)],
        compiler_params=_params("arbitrary"),
        name="mla_kv_context",
    )(ctx, mods, norm_w, kv_norm, w_down)
    return q_t, k, v_t, k_ctx, v_ctx_t


def _attn_kernel(q_ref, k_ref, v_ref, kc_ref, vc_ref, o_ref, *, tk):
    q_t = q_ref[0, 0, 0]
    tq = q_t.shape[1]
    n_chunks = v_ref.shape[1]

    def scores(c):
        return _dot(k_ref[0, c * tk:(c + 1) * tk, :], q_t)

    def pv(c, p):
        return _dot(v_ref[0, c], p)

    def finish(acc):
        o_ref[0] = (acc[0:V_HEAD, :] / acc[V_HEAD:V_HEAD + 1, :]).astype(BF16)

    def context_softmax():
        s = _dot(kc_ref[0], q_t)
        ref = jnp.max(s, axis=0, keepdims=True)
        return ref, jnp.exp2(s - ref).astype(BF16)

    ref, p_ctx = context_softmax()
    ahead = [scores(c) for c in range(min(2, n_chunks))]
    acc = _dot(vc_ref[0, 0], p_ctx)
    acc_ref = ref
    over = jnp.zeros((1, tq), F32)
    for c in range(n_chunks):
        s = ahead.pop(0)
        if c + 2 < n_chunks:
            ahead.append(scores(c + 2))
        chunk_max = jnp.max(s, axis=0, keepdims=True)
        p = jnp.exp2(s - ref).astype(BF16)
        acc = jnp.exp2(acc_ref - ref) * acc + pv(c, p)
        acc_ref = ref
        over = jnp.maximum(over, chunk_max - ref)
        ref = jnp.maximum(ref, chunk_max)
    finish(acc)

    @pl.when(jnp.max(over) > LAG_LIMIT)
    def _():
        def body(c, carry):
            m, acc = carry
            k0 = pl.multiple_of(c * tk, tk)
            s = _dot(k_ref[0, pl.ds(k0, tk), :], q_t)
            m_new = jnp.maximum(m, jnp.max(s, axis=0, keepdims=True))
            p = jnp.exp2(s - m_new).astype(BF16)
            acc = jnp.exp2(m - m_new) * acc + _dot(v_ref[0, c], p)
            return m_new, acc

        m0, p0 = context_softmax()
        finish(lax.fori_loop(0, n_chunks, body, (m0, _dot(vc_ref[0, 0], p0)))[1])


def _attention(q_t, k, v_t, k_ctx, v_ctx_t):
    b, nh, n_tiles, _, tq = q_t.shape
    n = n_tiles * tq
    n_chunks, tk = v_t.shape[1], v_t.shape[3]
    n_ctx = k_ctx.shape[1]
    batch3 = lambda bi, hi, i: (bi, 0, 0)
    batch4 = lambda bi, hi, i: (bi, 0, 0, 0)
    return pl.pallas_call(
        functools.partial(_attn_kernel, tk=tk),
        grid=(b, nh, n_tiles),
        in_specs=[
            pl.BlockSpec((1, 1, 1, QK_DIM, tq), lambda bi, hi, i: (bi, hi, i, 0, 0)),
            pl.BlockSpec((1, n, QK_DIM), batch3),
            pl.BlockSpec((1, n_chunks, V_ROWS, tk), batch4),
            pl.BlockSpec((1, n_ctx, QK_DIM), batch3),
            pl.BlockSpec((1, 1, V_ROWS, n_ctx), batch4),
        ],
        out_specs=pl.BlockSpec((1, V_HEAD, tq), lambda bi, hi, i: (bi, hi, i)),
        out_shape=jax.ShapeDtypeStruct((b, nh * V_HEAD, n), BF16),
        compiler_params=_params("arbitrary", "arbitrary", "arbitrary"),
        name="mla_attention",
    )(q_t, k, v_t, k_ctx, v_ctx_t)


def _ffn_kernel(*refs, with_proj, with_final):
    refs = list(refs)
    x_ref = refs.pop(0)
    o_ref = refs.pop(0) if with_proj else None
    mod_ref = refs.pop(0)
    nw_ref = refs.pop(0)
    wo_ref = refs.pop(0) if with_proj else None
    wg_ref, wu_ref, wd_ref = refs.pop(0), refs.pop(0), refs.pop(0)
    fw_ref = refs.pop(0) if with_final else None
    out_ref = refs.pop(0)

    x = x_ref[0]
    if with_proj:
        x = x + mod_ref[0, 0, 2:3, :] * _dot_tn(o_ref[0], wo_ref[0])
    shift, scale, gate = mod_ref[0, 0, 3:4, :], mod_ref[0, 0, 4:5, :], mod_ref[0, 0, 5:6, :]
    h = _rms_mod(x, nw_ref[...], shift, scale).astype(BF16)
    y = None
    for c0 in range(0, wg_ref.shape[2], FF_CHUNK):
        g = _dot(h, wg_ref[0, :, c0:c0 + FF_CHUNK])
        u = _dot(h, wu_ref[0, :, c0:c0 + FF_CHUNK])
        act = (g * jax.nn.sigmoid(g) * u).astype(BF16)
        part = _dot(act, wd_ref[0, c0:c0 + FF_CHUNK, :])
        y = part if y is None else y + part
    x = x + gate * y
    if with_final:
        x = _rms(x, fw_ref[...])
    out_ref[0] = x


def _ffn(x, o_t, mods, layer, norm_w, w_o, w_gate, w_up, w_down, final_w, tm):
    b, n, d = x.shape
    with_proj = o_t is not None
    with_final = final_w is not None
    tile = pl.BlockSpec((1, tm, d), lambda bi, i: (bi, i, 0))
    args, specs = [x], [tile]
    if with_proj:
        args.append(o_t)
        specs.append(pl.BlockSpec((1, d, tm), lambda bi, i: (bi, 0, i)))
    args += [mods, norm_w]
    specs += [_mod_spec(layer, lambda bi, i: bi), _resident((1, d))]
    if with_proj:
        args.append(w_o)
        specs.append(_resident_layer(w_o.shape, 0))
    args += [w_gate, w_up, w_down]
    specs += [_resident_layer(w.shape, layer) for w in (w_gate, w_up, w_down)]
    if with_final:
        args.append(final_w)
        specs.append(_resident((1, d)))
    return pl.pallas_call(
        functools.partial(_ffn_kernel, with_proj=with_proj, with_final=with_final),
        grid=(b, n // tm),
        in_specs=specs,
        out_specs=tile,
        out_shape=jax.ShapeDtypeStruct((b, n, d), F32),
        compiler_params=_params("arbitrary", "arbitrary"),
        name="attn_proj_ffn" if with_proj else "ffn_final",
    )(*args)


def _conv_kernel(x_ref, xp_ref, xn_ref, mod_ref, nw_ref, win_ref, cw_ref, wout_ref, out_ref, z_ref):
    i = pl.program_id(1)
    last = pl.num_programs(1) - 1
    d = x_ref.shape[2]
    tm = x_ref.shape[1]
    x = x_ref[0]
    xe = jnp.concatenate([xp_ref[0], x, xn_ref[0]], axis=0)
    shift, scale, gate = mod_ref[0, 0, 0:1, :], mod_ref[0, 0, 1:2, :], mod_ref[0, 0, 2:3, :]
    h = _rms_mod(xe, nw_ref[...], shift, scale).astype(BF16)
    p = _dot(h, win_ref[0])
    z = p[:, d:2 * d] * p[:, 2 * d:3 * d]
    row = lax.broadcasted_iota(jnp.int32, (tm + 2 * HALO, 1), 0)
    outside = ((row < HALO) & (i == 0)) | ((row >= tm + HALO) & (i == last))
    z_ref[...] = jnp.where(outside, 0.0, z)
    u = (cw_ref[0:1, :] * z_ref[HALO - 1:HALO - 1 + tm, :]
         + cw_ref[1:2, :] * z_ref[HALO:HALO + tm, :]
         + cw_ref[2:3, :] * z_ref[HALO + 1:HALO + 1 + tm, :])
    y = _dot((p[HALO:HALO + tm, 0:d] * u).astype(BF16), wout_ref[0])
    out_ref[0] = x + gate * y


def _short_conv(x, mods, layer, norm_w, w_in, conv_w, w_out, tm):
    b, n, d = x.shape
    nblk = n // HALO
    per = tm // HALO
    return pl.pallas_call(
        _conv_kernel,
        grid=(b, n // tm),
        in_specs=[
            pl.BlockSpec((1, tm, d), lambda bi, i: (bi, i, 0)),
            pl.BlockSpec((1, HALO, d), lambda bi, i: (bi, jnp.maximum(i * per - 1, 0), 0)),
            pl.BlockSpec((1, HALO, d), lambda bi, i: (bi, jnp.minimum((i + 1) * per, nblk - 1), 0)),
            _mod_spec(layer, lambda bi, i: bi),
            _resident((1, d)),
            _resident_layer(w_in.shape, 0),
            _resident(conv_w.shape),
            _resident_layer(w_out.shape, 0),
        ],
        out_specs=pl.BlockSpec((1, tm, d), lambda bi, i: (bi, i, 0)),
        out_shape=jax.ShapeDtypeStruct((b, n, d), F32),
        scratch_shapes=[pltpu.VMEM((tm + 2 * HALO, d), F32)],
        compiler_params=_params("arbitrary", "arbitrary"),
        name="short_conv",
    )(x, x, x, mods, norm_w, w_in, conv_w, w_out)


def _rope_tables(n):
    rows = n // GRID_W
    n_freq = QK_ROPE // 4
    freqs = ROPE_THETA ** (-jnp.arange(n_freq, dtype=F32) / n_freq)
    ar = jnp.arange(rows).astype(F32)[:, None] * freqs
    ac = jnp.arange(GRID_W).astype(F32)[:, None] * freqs

    def table(fn):
        r = jnp.broadcast_to(fn(ar)[:, None, :], (rows, GRID_W, n_freq)).reshape(n, n_freq)
        c = jnp.broadcast_to(fn(ac)[None, :, :], (rows, GRID_W, n_freq)).reshape(n, n_freq)
        return jnp.concatenate([r, r, c, c], axis=-1)

    return table(jnp.cos), table(jnp.sin)


def _rotate_cols(w):
    q = QK_ROPE // 4
    perm = np.concatenate([np.arange(q, 2 * q), np.arange(0, q), np.arange(3 * q, 4 * q), np.arange(2 * q, 3 * q)])
    sign = np.concatenate([-np.ones(q), np.ones(q), -np.ones(q), np.ones(q)]).astype(np.float32)
    return w[..., perm] * sign


def kernel(x, c, ctx, c_ctx, ada_w, ada_b, norm_mix, norm_ffn, mla_w_dq, mla_q_norm, mla_w_uq, mla_w_dkv,
           mla_kv_norm, mla_w_ukv, mla_w_o, conv_w_in, conv_w, conv_w_out, ffn_w_gate, ffn_w_up, ffn_w_down,
           final_norm):
    b, n, d = x.shape

    pad = (-(b + 1)) % 8
    c_rows = jnp.concatenate([c, c_ctx[None, :], jnp.zeros((pad, d), F32)], axis=0)
    mods = _modulation(c_rows, ada_w, ada_b)
    mods = mods.reshape(mods.shape[0], mods.shape[1], 6, d)

    w_uq = mla_w_uq[0].reshape(Q_LORA, N_HEADS, QK_DIM)
    w_ukv = mla_w_ukv[0].reshape(KV_LORA, N_HEADS, QK_NOPE + V_HEAD).transpose(1, 0, 2)
    w_qabs_t, w_o_fold = _fold_weights(w_uq[:, :, :QK_NOPE].transpose(1, 0, 2), w_ukv[:, :, :QK_NOPE],
                                       w_ukv[:, :, QK_NOPE:], mla_w_o[0].reshape(N_HEADS, V_HEAD, d))
    uq_rope = w_uq[:, :, QK_NOPE:]
    w_uq_t = jnp.concatenate([w_qabs_t.reshape(N_HEADS * KV_LORA, Q_LORA), uq_rope.reshape(Q_LORA, -1).T,
                              _rotate_cols(uq_rope).reshape(Q_LORA, -1).T], axis=0).astype(BF16)
    w_dkv = mla_w_dkv[0]
    w_down = jnp.concatenate([mla_w_dq[0], w_dkv, _rotate_cols(w_dkv[:, KV_LORA:])], axis=1).astype(BF16)
    w_o_fold = w_o_fold.reshape(1, N_HEADS * KV_LORA, d).astype(BF16)
    tc, ts = _rope_tables(n)
    tct, tst = tc.T, ts.T

    q_t, k, v_t, k_ctx, v_ctx_t = _mla_qkv(x, ctx, mods, norm_mix[0][None], mla_q_norm[0][None],
                                           mla_kv_norm[0][None], tc, ts, tct, tst, w_down, w_uq_t,
                                           tq=min(ATTN_TQ, n), tk=min(ATTN_TK, n))
    o_t = _attention(q_t, k, v_t, k_ctx, v_ctx_t)
    w_gate, w_up, w_dn = ffn_w_gate.astype(BF16), ffn_w_up.astype(BF16), ffn_w_down.astype(BF16)
    x = _ffn(x, o_t, mods, 0, norm_ffn[0][None], w_o_fold, w_gate, w_up, w_dn, None, tm=min(1024, n))

    x = _short_conv(x, mods, 1, norm_mix[1][None], conv_w_in.astype(BF16), conv_w[0], conv_w_out.astype(BF16),
                    tm=min(1024, n))
    return _ffn(x, None, mods, 1, norm_ffn[1][None], None, w_gate, w_up, w_dn, final_norm[None], tm=min(1024, n))
```

```python
import functools
import math

import numpy as np
import jax
import jax.numpy as jnp
from jax import lax
from jax.experimental import pallas as pl
from jax.experimental.pallas import tpu as pltpu

D_MODEL = 1024
GRID_W = 64
N_HEADS = 8
QK_NOPE = 128
QK_ROPE = 64
V_HEAD = 128
Q_LORA = 256
KV_LORA = 128
QK_DIM = QK_NOPE + QK_ROPE
V_ROWS = V_HEAD + 16
ROPE_THETA = 10000.0
NORM_EPS = 1e-6
ATTN_SCALE = 1.0 / math.sqrt(QK_NOPE + QK_ROPE)
LOG2_E = 1.4426950408889634
LAG_LIMIT = 60.0
ATTN_TQ = 1024
ATTN_TK = 512
ATTN_TILES_PER_STEP = 2
FF_CHUNK = 256

F32 = jnp.float32
BF16 = jnp.bfloat16

VMEM_LIMIT_BYTES = 60 * 1024 * 1024
HALO = 8


def _params(*sem):
    return pltpu.CompilerParams(dimension_semantics=sem, vmem_limit_bytes=VMEM_LIMIT_BYTES)


def _resident(shape):
    zeros = (0,) * len(shape)
    return pl.BlockSpec(shape, lambda *_: zeros, pipeline_mode=pl.Buffered(1))


def _resident_layer(shape, layer):
    rest = (0,) * (len(shape) - 1)
    return pl.BlockSpec((1,) + tuple(shape[1:]), lambda *_: (layer,) + rest, pipeline_mode=pl.Buffered(1))


def _mod_spec(layer, row_of):
    return pl.BlockSpec((1, 1, 6, D_MODEL), lambda *idx: (layer, row_of(*idx), 0, 0))


def _rms(x, w):
    return x * lax.rsqrt(jnp.mean(x * x, axis=-1, keepdims=True) + NORM_EPS) * w


def _rms_mod(x, w, shift, scale):
    return x * lax.rsqrt(jnp.mean(x * x, axis=-1, keepdims=True) + NORM_EPS) * (w * (1.0 + scale)) + shift


def _dot(a, b):
    return jnp.dot(a, b, preferred_element_type=F32)


def _dot_tn(a, b):
    return lax.dot_general(a, b, (((0,), (0,)), ((), ())), preferred_element_type=F32)


def _dot_nt(a, b):
    return lax.dot_general(a, b, (((1,), (1,)), ((), ())), preferred_element_type=F32)


def _mod_kernel(c_ref, w_ref, b_ref, o_ref):
    c = c_ref[...]
    o_ref[0] = _dot(c * jax.nn.sigmoid(c), w_ref[0]) + b_ref[0]


def _modulation(c_rows, ada_w, ada_b):
    depth, d, d6 = ada_w.shape
    rows = c_rows.shape[0]
    tn = 1536
    return pl.pallas_call(
        _mod_kernel,
        grid=(depth, d6 // tn),
        in_specs=[
            pl.BlockSpec((rows, d), lambda i, n: (0, 0)),
            pl.BlockSpec((1, d, tn), lambda i, n: (i, 0, n)),
            pl.BlockSpec((1, 1, tn), lambda i, n: (i, 0, n)),
        ],
        out_specs=pl.BlockSpec((1, rows, tn), lambda i, n: (i, 0, n)),
        out_shape=jax.ShapeDtypeStruct((depth, rows, d6), F32),
        compiler_params=_params("arbitrary", "arbitrary"),
        name="adaln_modulation",
    )(c_rows, ada_w, ada_b.reshape(depth, 1, d6))


def _fold_kernel(wqn_ref, wuk_ref, wuv_ref, wo_ref, qabs_ref, ofold_ref):
    hp = lax.Precision.HIGHEST
    for hd in range(wqn_ref.shape[0]):
        qabs_ref[hd] = lax.dot_general(wuk_ref[hd], wqn_ref[hd], (((1,), (1,)), ((), ())), precision=hp,
                                       preferred_element_type=F32)
        ofold_ref[hd] = jnp.dot(wuv_ref[hd], wo_ref[hd], precision=hp, preferred_element_type=F32)


def _fold_weights(w_qnope, w_uk, w_uv, w_o):
    nh, q_lora, _ = w_qnope.shape
    d = w_o.shape[2]
    group = nh // 2
    per_head = lambda shape: pl.BlockSpec((group,) + shape, lambda h: (h, 0, 0))
    return pl.pallas_call(
        _fold_kernel,
        grid=(nh // group,),
        in_specs=[per_head((q_lora, QK_NOPE)), per_head((KV_LORA, QK_NOPE)), per_head((KV_LORA, V_HEAD)),
                  per_head((V_HEAD, d))],
        out_specs=[per_head((KV_LORA, q_lora)), per_head((KV_LORA, d))],
        out_shape=[jax.ShapeDtypeStruct((nh, KV_LORA, q_lora), F32), jax.ShapeDtypeStruct((nh, KV_LORA, d), F32)],
        compiler_params=_params("arbitrary"),
        name="mla_fold_weights",
    )(w_qnope, w_uk, w_uv, w_o)


def _kv_store(ckv, k_rope, k_ref, v_ref):
    k_ref[0, :, 0:KV_LORA] = ckv.astype(BF16)
    k_ref[0, :, KV_LORA:QK_DIM] = k_rope
    ckv_t = ckv.T.astype(BF16)
    tk = v_ref.shape[3]
    ones = jnp.ones((V_ROWS - V_HEAD, tk), BF16)
    for cc in range(v_ref.shape[1]):
        v_ref[0, cc, 0:V_HEAD, :] = ckv_t[:, cc * tk:(cc + 1) * tk]
        v_ref[0, cc, V_HEAD:V_ROWS, :] = ones


def _qkv_latent_kernel(x_ref, mod_ref, nw_ref, qn_ref, kvn_ref, tc_ref, ts_ref, tct_ref, tst_ref,
                       wdn_ref, wuqt_ref, q_ref, k_ref, v_ref):
    shift, scale = mod_ref[0, 0, 0:1, :], mod_ref[0, 0, 1:2, :]
    h = _rms_mod(x_ref[0], nw_ref[...], shift, scale).astype(BF16)
    down = _dot(h, wdn_ref[...])
    ckv = _rms(down[:, Q_LORA:Q_LORA + KV_LORA], kvn_ref[...])
    kr0 = Q_LORA + KV_LORA
    k_rope = (down[:, kr0:kr0 + QK_ROPE] * tc_ref[...]
              + down[:, kr0 + QK_ROPE:kr0 + 2 * QK_ROPE] * ts_ref[...]).astype(BF16)
    _kv_store(ckv, k_rope, k_ref, v_ref)

    cq = _rms(down[:, 0:Q_LORA], qn_ref[...] * (ATTN_SCALE * LOG2_E)).astype(BF16)
    q_t = _dot_nt(wuqt_ref[...], cq)
    r0 = N_HEADS * QK_NOPE
    r1 = r0 + N_HEADS * QK_ROPE
    cos2 = jnp.concatenate([tct_ref[...], tct_ref[...]], axis=0)
    sin2 = jnp.concatenate([tst_ref[...], tst_ref[...]], axis=0)
    for pair in range(N_HEADS // 2):
        lo = 2 * QK_ROPE * pair
        rope2 = (q_t[r0 + lo:r0 + lo + 2 * QK_ROPE, :] * cos2
                 + q_t[r1 + lo:r1 + lo + 2 * QK_ROPE, :] * sin2).astype(BF16)
        for sub in range(2):
            hd = 2 * pair + sub
            q_ref[0, hd, 0, 0:QK_NOPE, :] = q_t[hd * QK_NOPE:(hd + 1) * QK_NOPE, :].astype(BF16)
            q_ref[0, hd, 0, QK_NOPE:QK_DIM, :] = rope2[sub * QK_ROPE:(sub + 1) * QK_ROPE, :]


def _kv_context_kernel(x_ref, mod_ref, nw_ref, kvn_ref, wdn_ref, k_ref, v_ref):
    shift, scale = mod_ref[0, 0, 0:1, :], mod_ref[0, 0, 1:2, :]
    h = _rms_mod(x_ref[0], nw_ref[...], shift, scale).astype(BF16)
    down = _dot(h, wdn_ref[...])
    ckv = _rms(down[:, Q_LORA:Q_LORA + KV_LORA], kvn_ref[...])
    kr0 = Q_LORA + KV_LORA
    k_rope = down[:, kr0:kr0 + QK_ROPE].astype(BF16)
    _kv_store(ckv, k_rope, k_ref, v_ref)


def _mla_qkv(x, ctx, mods, norm_w, q_norm, kv_norm, tc, ts, tct, tst, w_down, w_uq_t, tq, tk):
    b, n, d = x.shape
    n_ctx = ctx.shape[1]
    tm = tq
    assert n % tm == 0 and tm % tk == 0
    q_t, k, v_t = pl.pallas_call(
        _qkv_latent_kernel,
        grid=(n // tm, b),
        in_specs=[
            pl.BlockSpec((1, tm, d), lambda j, bi: (bi, j, 0)),
            _mod_spec(0, lambda j, bi: bi),
            _resident((1, d)),
            _resident((1, Q_LORA)),
            _resident((1, KV_LORA)),
            pl.BlockSpec((tm, QK_ROPE), lambda j, bi: (j, 0)),
            pl.BlockSpec((tm, QK_ROPE), lambda j, bi: (j, 0)),
            pl.BlockSpec((QK_ROPE, tm), lambda j, bi: (0, j)),
            pl.BlockSpec((QK_ROPE, tm), lambda j, bi: (0, j)),
            _resident(w_down.shape),
            _resident(w_uq_t.shape),
        ],
        out_specs=[
            pl.BlockSpec((1, N_HEADS, 1, QK_DIM, tm), lambda j, bi: (bi, 0, j, 0, 0)),
            pl.BlockSpec((1, tm, QK_DIM), lambda j, bi: (bi, j, 0)),
            pl.BlockSpec((1, tm // tk, V_ROWS, tk), lambda j, bi: (bi, j, 0, 0)),
        ],
        out_shape=[jax.ShapeDtypeStruct((b, N_HEADS, n // tq, QK_DIM, tq), BF16),
                   jax.ShapeDtypeStruct((b, n, QK_DIM), BF16),
                   jax.ShapeDtypeStruct((b, n // tk, V_ROWS, tk), BF16)],
        compiler_params=_params("arbitrary", "arbitrary"),
        name="mla_qkv_latent",
    )(x, mods, norm_w, q_norm, kv_norm, tc, ts, tct, tst, w_down, w_uq_t)
    k_ctx, v_ctx_t = pl.pallas_call(
        _kv_context_kernel,
        grid=(b,),
        in_specs=[
            pl.BlockSpec((1, n_ctx, d), lambda bi: (bi, 0, 0)),
            _mod_spec(0, lambda bi: b),
            _resident((1, d)),
            _resident((1, KV_LORA)),
            _resident(w_down.shape),
        ],
        out_specs=[
            pl.BlockSpec((1, n_ctx, QK_DIM), lambda bi: (bi, 0, 0)),
            pl.BlockSpec((1, 1, V_ROWS, n_ctx), lambda bi: (bi, 0, 0, 0)),
        ],
        out_shape=[jax.ShapeDtypeStruct((b, n_ctx, QK_DIM), BF16),
                   jax.ShapeDtypeStruct((b, 1, V_ROWS, n_ctx), BF16)],
        compiler_params=_params("arbitrary"),
        name="mla_kv_context",
    )(ctx, mods, norm_w, kv_norm, w_down)
    return q_t, k, v_t, k_ctx, v_ctx_t


def _attn_kernel(q_ref, k_ref, v_ref, kc_ref, vc_ref, o_ref, *, tk):
    n_tiles, tq = q_ref.shape[2], q_ref.shape[4]
    n_chunks = v_ref.shape[1]

    def scores(t, c):
        keys = kc_ref[0] if c < 0 else k_ref[0, c * tk:(c + 1) * tk, :]
        return _dot(keys, q_ref[0, 0, t])

    def values(c):
        return vc_ref[0, 0] if c < 0 else v_ref[0, c]

    def finish(t, acc):
        o_ref[0, :, t * tq:(t + 1) * tq] = (acc[0:V_HEAD, :] / acc[V_HEAD:V_HEAD + 1, :]).astype(BF16)

    items = [(t, c) for t in range(n_tiles) for c in range(-1, n_chunks)]
    ahead = [scores(*it) for it in items[:2]]
    over_by_tile = []
    for idx, (t, c) in enumerate(items):
        s = ahead.pop(0)
        if idx + 2 < len(items):
            ahead.append(scores(*items[idx + 2]))
        chunk_max = jnp.max(s, axis=0, keepdims=True)
        if c < 0:
            ref = chunk_max
            acc = _dot(values(c), jnp.exp2(s - ref).astype(BF16))
            over = jnp.zeros((1, tq), F32)
        else:
            p = jnp.exp2(s - ref).astype(BF16)
            acc = jnp.exp2(acc_ref - ref) * acc + _dot(values(c), p)
            over = jnp.maximum(over, chunk_max - ref)
        acc_ref = ref
        ref = jnp.maximum(ref, chunk_max)
        if c == n_chunks - 1:
            finish(t, acc)
            over_by_tile.append(over)

    for t, over in enumerate(over_by_tile):
        @pl.when(jnp.max(over) > LAG_LIMIT)
        def _(t=t):
            q_t = q_ref[0, 0, t]

            def body(c, carry):
                m, acc = carry
                k0 = pl.multiple_of(c * tk, tk)
                s = _dot(k_ref[0, pl.ds(k0, tk), :], q_t)
                m_new = jnp.maximum(m, jnp.max(s, axis=0, keepdims=True))
                p = jnp.exp2(s - m_new).astype(BF16)
                acc = jnp.exp2(m - m_new) * acc + _dot(v_ref[0, c], p)
                return m_new, acc

            s0 = _dot(kc_ref[0], q_t)
            m0 = jnp.max(s0, axis=0, keepdims=True)
            acc0 = _dot(vc_ref[0, 0], jnp.exp2(s0 - m0).astype(BF16))
            finish(t, lax.fori_loop(0, n_chunks, body, (m0, acc0))[1])


def _attention(q_t, k, v_t, k_ctx, v_ctx_t):
    b, nh, n_tiles, _, tq = q_t.shape
    n = n_tiles * tq
    per_step = math.gcd(ATTN_TILES_PER_STEP, n_tiles)
    n_chunks, tk = v_t.shape[1], v_t.shape[3]
    n_ctx = k_ctx.shape[1]
    batch3 = lambda bi, hi, i: (bi, 0, 0)
    batch4 = lambda bi, hi, i: (bi, 0, 0, 0)
    return pl.pallas_call(
        functools.partial(_attn_kernel, tk=tk),
        grid=(b, nh, n_tiles // per_step),
        in_specs=[
            pl.BlockSpec((1, 1, per_step, QK_DIM, tq), lambda bi, hi, i: (bi, hi, i, 0, 0)),
            pl.BlockSpec((1, n, QK_DIM), batch3),
            pl.BlockSpec((1, n_chunks, V_ROWS, tk), batch4),
            pl.BlockSpec((1, n_ctx, QK_DIM), batch3),
            pl.BlockSpec((1, 1, V_ROWS, n_ctx), batch4),
        ],
        out_specs=pl.BlockSpec((1, V_HEAD, per_step * tq), lambda bi, hi, i: (bi, hi, i)),
        out_shape=jax.ShapeDtypeStruct((b, nh * V_HEAD, n), BF16),
        compiler_params=_params("arbitrary", "arbitrary", "arbitrary"),
        name="mla_attention",
    )(q_t, k, v_t, k_ctx, v_ctx_t)


def _ffn_kernel(*refs, with_proj, with_final):
    refs = list(refs)
    x_ref = refs.pop(0)
    o_ref = refs.pop(0) if with_proj else None
    mod_ref = refs.pop(0)
    nw_ref = refs.pop(0)
    wo_ref = refs.pop(0) if with_proj else None
    wg_ref, wu_ref, wd_ref = refs.pop(0), refs.pop(0), refs.pop(0)
    fw_ref = refs.pop(0) if with_final else None
    out_ref = refs.pop(0)

    x = x_ref[0]
    if with_proj:
        x = x + mod_ref[0, 0, 2:3, :] * _dot_tn(o_ref[0], wo_ref[0])
    shift, scale, gate = mod_ref[0, 0, 3:4, :], mod_ref[0, 0, 4:5, :], mod_ref[0, 0, 5:6, :]
    h = _rms_mod(x, nw_ref[...], shift, scale).astype(BF16)
    y = None
    for c0 in range(0, wg_ref.shape[2], FF_CHUNK):
        g = _dot(h, wg_ref[0, :, c0:c0 + FF_CHUNK])
        u = _dot(h, wu_ref[0, :, c0:c0 + FF_CHUNK])
        act = (g * jax.nn.sigmoid(g) * u).astype(BF16)
        part = _dot(act, wd_ref[0, c0:c0 + FF_CHUNK, :])
        y = part if y is None else y + part
    x = x + gate * y
    if with_final:
        x = _rms(x, fw_ref[...])
    out_ref[0] = x


def _ffn(x, o_t, mods, layer, norm_w, w_o, w_gate, w_up, w_down, final_w, tm):
    b, n, d = x.shape
    with_proj = o_t is not None
    with_final = final_w is not None
    tile = pl.BlockSpec((1, tm, d), lambda bi, i: (bi, i, 0))
    args, specs = [x], [tile]
    if with_proj:
        args.append(o_t)
        specs.append(pl.BlockSpec((1, d, tm), lambda bi, i: (bi, 0, i)))
    args += [mods, norm_w]
    specs += [_mod_spec(layer, lambda bi, i: bi), _resident((1, d))]
    if with_proj:
        args.append(w_o)
        specs.append(_resident_layer(w_o.shape, 0))
    args += [w_gate, w_up, w_down]
    specs += [_resident_layer(w.shape, layer) for w in (w_gate, w_up, w_down)]
    if with_final:
        args.append(final_w)
        specs.append(_resident((1, d)))
    return pl.pallas_call(
        functools.partial(_ffn_kernel, with_proj=with_proj, with_final=with_final),
        grid=(b, n // tm),
        in_specs=specs,
        out_specs=tile,
        out_shape=jax.ShapeDtypeStruct((b, n, d), F32),
        compiler_params=_params("arbitrary", "arbitrary"),
        name="attn_proj_ffn" if with_proj else "ffn_final",
    )(*args)


def _conv_kernel(x_ref, xp_ref, xn_ref, mod_ref, nw_ref, win_ref, cw_ref, wout_ref, out_ref, z_ref):
    i = pl.program_id(1)
    last = pl.num_programs(1) - 1
    d = x_ref.shape[2]
    tm = x_ref.shape[1]
    x = x_ref[0]
    xe = jnp.concatenate([xp_ref[0], x, xn_ref[0]], axis=0)
    shift, scale, gate = mod_ref[0, 0, 0:1, :], mod_ref[0, 0, 1:2, :], mod_ref[0, 0, 2:3, :]
    h = _rms_mod(xe, nw_ref[...], shift, scale).astype(BF16)
    p = _dot(h, win_ref[0])
    z = p[:, d:2 * d] * p[:, 2 * d:3 * d]
    row = lax.broadcasted_iota(jnp.int32, (tm + 2 * HALO, 1), 0)
    outside = ((row < HALO) & (i == 0)) | ((row >= tm + HALO) & (i == last))
    z_ref[...] = jnp.where(outside, 0.0, z)
    u = (cw_ref[0:1, :] * z_ref[HALO - 1:HALO - 1 + tm, :]
         + cw_ref[1:2, :] * z_ref[HALO:HALO + tm, :]
         + cw_ref[2:3, :] * z_ref[HALO + 1:HALO + 1 + tm, :])
    y = _dot((p[HALO:HALO + tm, 0:d] * u).astype(BF16), wout_ref[0])
    out_ref[0] = x + gate * y


def _short_conv(x, mods, layer, norm_w, w_in, conv_w, w_out, tm):
    b, n, d = x.shape
    nblk = n // HALO
    per = tm // HALO
    return pl.pallas_call(
        _conv_kernel,
        grid=(b, n // tm),
        in_specs=[
            pl.BlockSpec((1, tm, d), lambda bi, i: (bi, i, 0)),
            pl.BlockSpec((1, HALO, d), lambda bi, i: (bi, jnp.maximum(i * per - 1, 0), 0)),
            pl.BlockSpec((1, HALO, d), lambda bi, i: (bi, jnp.minimum((i + 1) * per, nblk - 1), 0)),
            _mod_spec(layer, lambda bi, i: bi),
            _resident((1, d)),
            _resident_layer(w_in.shape, 0),
            _resident(conv_w.shape),
            _resident_layer(w_out.shape, 0),
        ],
        out_specs=pl.BlockSpec((1, tm, d), lambda bi, i: (bi, i, 0)),
        out_shape=jax.ShapeDtypeStruct((b, n, d), F32),
        scratch_shapes=[pltpu.VMEM((tm + 2 * HALO, d), F32)],
        compiler_params=_params("arbitrary", "arbitrary"),
        name="short_conv",
    )(x, x, x, mods, norm_w, w_in, conv_w, w_out)


def _rope_tables(n):
    rows = n // GRID_W
    n_freq = QK_ROPE // 4
    freqs = ROPE_THETA ** (-jnp.arange(n_freq, dtype=F32) / n_freq)
    ar = jnp.arange(rows).astype(F32)[:, None] * freqs
    ac = jnp.arange(GRID_W).astype(F32)[:, None] * freqs

    def table(fn):
        r = jnp.broadcast_to(fn(ar)[:, None, :], (rows, GRID_W, n_freq)).reshape(n, n_freq)
        c = jnp.broadcast_to(fn(ac)[None, :, :], (rows, GRID_W, n_freq)).reshape(n, n_freq)
        return jnp.concatenate([r, r, c, c], axis=-1)

    return table(jnp.cos), table(jnp.sin)


def _rotate_cols(w):
    q = QK_ROPE // 4
    perm = np.concatenate([np.arange(q, 2 * q), np.arange(0, q), np.arange(3 * q, 4 * q), np.arange(2 * q, 3 * q)])
    sign = np.concatenate([-np.ones(q), np.ones(q), -np.ones(q), np.ones(q)]).astype(np.float32)
    return w[..., perm] * sign


def kernel(x, c, ctx, c_ctx, ada_w, ada_b, norm_mix, norm_ffn, mla_w_dq, mla_q_norm, mla_w_uq, mla_w_dkv,
           mla_kv_norm, mla_w_ukv, mla_w_o, conv_w_in, conv_w, conv_w_out, ffn_w_gate, ffn_w_up, ffn_w_down,
           final_norm):
    b, n, d = x.shape

    pad = (-(b + 1)) % 8
    c_rows = jnp.concatenate([c, c_ctx[None, :], jnp.zeros((pad, d), F32)], axis=0)
    mods = _modulation(c_rows, ada_w, ada_b)
    mods = mods.reshape(mods.shape[0], mods.shape[1], 6, d)

    w_uq = mla_w_uq[0].reshape(Q_LORA, N_HEADS, QK_DIM)
    w_ukv = mla_w_ukv[0].reshape(KV_LORA, N_HEADS, QK_NOPE + V_HEAD).transpose(1, 0, 2)
    w_qabs_t, w_o_fold = _fold_weights(w_uq[:, :, :QK_NOPE].transpose(1, 0, 2), w_ukv[:, :, :QK_NOPE],
                                       w_ukv[:, :, QK_NOPE:], mla_w_o[0].reshape(N_HEADS, V_HEAD, d))
    uq_rope = w_uq[:, :, QK_NOPE:]
    w_uq_t = jnp.concatenate([w_qabs_t.reshape(N_HEADS * KV_LORA, Q_LORA), uq_rope.reshape(Q_LORA, -1).T,
                              _rotate_cols(uq_rope).reshape(Q_LORA, -1).T], axis=0).astype(BF16)
    w_dkv = mla_w_dkv[0]
    w_down = jnp.concatenate([mla_w_dq[0], w_dkv, _rotate_cols(w_dkv[:, KV_LORA:])], axis=1).astype(BF16)
    w_o_fold = w_o_fold.reshape(1, N_HEADS * KV_LORA, d).astype(BF16)
    tc, ts = _rope_tables(n)
    tct, tst = tc.T, ts.T

    q_t, k, v_t, k_ctx, v_ctx_t = _mla_qkv(x, ctx, mods, norm_mix[0][None], mla_q_norm[0][None],
                                           mla_kv_norm[0][None], tc, ts, tct, tst, w_down, w_uq_t,
                                           tq=min(ATTN_TQ, n), tk=min(ATTN_TK, n))
    o_t = _attention(q_t, k, v_t, k_ctx, v_ctx_t)
    w_gate, w_up, w_dn = ffn_w_gate.astype(BF16), ffn_w_up.astype(BF16), ffn_w_down.astype(BF16)
    x = _ffn(x, o_t, mods, 0, norm_ffn[0][None], w_o_fold, w_gate, w_up, w_dn, None, tm=min(1024, n))

    x = _short_conv(x, mods, 1, norm_mix[1][None], conv_w_in.astype(BF16), conv_w[0], conv_w_out.astype(BF16),
                    tm=min(1024, n))
    return _ffn(x, None, mods, 1, norm_ffn[1][None], None, w_gate, w_up, w_dn, final_norm[None], tm=min(1024, n))
```

```python
import functools
import math

import numpy as np
import jax
import jax.numpy as jnp
from jax import lax
from jax.experimental import pallas as pl
from jax.experimental.pallas import tpu as pltpu

D_MODEL = 1024
GRID_W = 64
N_HEADS = 8
QK_NOPE = 128
QK_ROPE = 64
V_HEAD = 128
Q_LORA = 256
KV_LORA = 128
QK_DIM = QK_NOPE + QK_ROPE
V_ROWS = V_HEAD + 16
ROPE_THETA = 10000.0
NORM_EPS = 1e-6
ATTN_SCALE = 1.0 / math.sqrt(QK_NOPE + QK_ROPE)
LOG2_E = 1.4426950408889634
LAG_LIMIT = 60.0
ATTN_TQ = 1024
ATTN_TK = 512
ATTN_TILES_PER_STEP = 2
FF_CHUNK = 256

F32 = jnp.float32
BF16 = jnp.bfloat16

VMEM_LIMIT_BYTES = 60 * 1024 * 1024
HALO = 8


def _params(*sem):
    return pltpu.CompilerParams(dimension_semantics=sem, vmem_limit_bytes=VMEM_LIMIT_BYTES)


def _resident(shape):
    zeros = (0,) * len(shape)
    return pl.BlockSpec(shape, lambda *_: zeros, pipeline_mode=pl.Buffered(1))


def _resident_layer(shape, layer):
    rest = (0,) * (len(shape) - 1)
    return pl.BlockSpec((1,) + tuple(shape[1:]), lambda *_: (layer,) + rest, pipeline_mode=pl.Buffered(1))


def _mod_spec(layer, row_of):
    return pl.BlockSpec((1, 1, 6, D_MODEL), lambda *idx: (layer, row_of(*idx), 0, 0))


def _rms(x, w):
    return x * lax.rsqrt(jnp.mean(x * x, axis=-1, keepdims=True) + NORM_EPS) * w


def _rms_mod(x, w, shift, scale):
    return x * lax.rsqrt(jnp.mean(x * x, axis=-1, keepdims=True) + NORM_EPS) * (w * (1.0 + scale)) + shift


def _dot(a, b):
    return jnp.dot(a, b, preferred_element_type=F32)


def _dot_tn(a, b):
    return lax.dot_general(a, b, (((0,), (0,)), ((), ())), preferred_element_type=F32)


def _dot_nt(a, b):
    return lax.dot_general(a, b, (((1,), (1,)), ((), ())), preferred_element_type=F32)


def _mod_kernel(c_ref, w_ref, b_ref, o_ref):
    c = c_ref[...]
    o_ref[0] = _dot(c * jax.nn.sigmoid(c), w_ref[0]) + b_ref[0]


def _modulation(c_rows, ada_w, ada_b):
    depth, d, d6 = ada_w.shape
    rows = c_rows.shape[0]
    tn = 1536
    return pl.pallas_call(
        _mod_kernel,
        grid=(depth, d6 // tn),
        in_specs=[
            pl.BlockSpec((rows, d), lambda i, n: (0, 0)),
            pl.BlockSpec((1, d, tn), lambda i, n: (i, 0, n)),
            pl.BlockSpec((1, 1, tn), lambda i, n: (i, 0, n)),
        ],
        out_specs=pl.BlockSpec((1, rows, tn), lambda i, n: (i, 0, n)),
        out_shape=jax.ShapeDtypeStruct((depth, rows, d6), F32),
        compiler_params=_params("arbitrary", "arbitrary"),
        name="adaln_modulation",
    )(c_rows, ada_w, ada_b.reshape(depth, 1, d6))


def _fold_kernel(wqn_ref, wuk_ref, wuv_ref, wo_ref, qabs_ref, ofold_ref):
    hp = lax.Precision.HIGHEST
    for hd in range(wqn_ref.shape[0]):
        qabs_ref[hd] = lax.dot_general(wuk_ref[hd], wqn_ref[hd], (((1,), (1,)), ((), ())), precision=hp,
                                       preferred_element_type=F32)
        ofold_ref[hd] = jnp.dot(wuv_ref[hd], wo_ref[hd], precision=hp, preferred_element_type=F32)


def _fold_weights(w_qnope, w_uk, w_uv, w_o):
    nh, q_lora, _ = w_qnope.shape
    d = w_o.shape[2]
    group = nh // 2
    per_head = lambda shape: pl.BlockSpec((group,) + shape, lambda h: (h, 0, 0))
    return pl.pallas_call(
        _fold_kernel,
        grid=(nh // group,),
        in_specs=[per_head((q_lora, QK_NOPE)), per_head((KV_LORA, QK_NOPE)), per_head((KV_LORA, V_HEAD)),
                  per_head((V_HEAD, d))],
        out_specs=[per_head((KV_LORA, q_lora)), per_head((KV_LORA, d))],
        out_shape=[jax.ShapeDtypeStruct((nh, KV_LORA, q_lora), F32), jax.ShapeDtypeStruct((nh, KV_LORA, d), F32)],
        compiler_params=_params("arbitrary"),
        name="mla_fold_weights",
    )(w_qnope, w_uk, w_uv, w_o)


def _kv_store(ckv, k_rope, k_ref, v_ref):
    k_ref[0, :, 0:KV_LORA] = ckv.astype(BF16)
    k_ref[0, :, KV_LORA:QK_DIM] = k_rope
    ckv_t = ckv.T.astype(BF16)
    tk = v_ref.shape[3]
    ones = jnp.ones((V_ROWS - V_HEAD, tk), BF16)
    for cc in range(v_ref.shape[1]):
        v_ref[0, cc, 0:V_HEAD, :] = ckv_t[:, cc * tk:(cc + 1) * tk]
        v_ref[0, cc, V_HEAD:V_ROWS, :] = ones


def _qkv_latent_kernel(x_ref, mod_ref, nw_ref, qn_ref, kvn_ref, tc_ref, ts_ref, tct_ref, tst_ref,
                       wdn_ref, wuqt_ref, q_ref, k_ref, v_ref):
    shift, scale = mod_ref[0, 0, 0:1, :], mod_ref[0, 0, 1:2, :]
    h = _rms_mod(x_ref[0], nw_ref[...], shift, scale).astype(BF16)
    down = _dot(h, wdn_ref[...])
    ckv = _rms(down[:, Q_LORA:Q_LORA + KV_LORA], kvn_ref[...])
    kr0 = Q_LORA + KV_LORA
    k_rope = (down[:, kr0:kr0 + QK_ROPE] * tc_ref[...]
              + down[:, kr0 + QK_ROPE:kr0 + 2 * QK_ROPE] * ts_ref[...]).astype(BF16)
    _kv_store(ckv, k_rope, k_ref, v_ref)

    cq = _rms(down[:, 0:Q_LORA], qn_ref[...] * (ATTN_SCALE * LOG2_E)).astype(BF16)
    q_t = _dot_nt(wuqt_ref[...], cq)
    r0 = N_HEADS * QK_NOPE
    r1 = r0 + N_HEADS * QK_ROPE
    cos2 = jnp.concatenate([tct_ref[...], tct_ref[...]], axis=0)
    sin2 = jnp.concatenate([tst_ref[...], tst_ref[...]], axis=0)
    for pair in range(N_HEADS // 2):
        lo = 2 * QK_ROPE * pair
        rope2 = (q_t[r0 + lo:r0 + lo + 2 * QK_ROPE, :] * cos2
                 + q_t[r1 + lo:r1 + lo + 2 * QK_ROPE, :] * sin2).astype(BF16)
        for sub in range(2):
            hd = 2 * pair + sub
            q_ref[0, hd, 0, 0:QK_NOPE, :] = q_t[hd * QK_NOPE:(hd + 1) * QK_NOPE, :].astype(BF16)
            q_ref[0, hd, 0, QK_NOPE:QK_DIM, :] = rope2[sub * QK_ROPE:(sub + 1) * QK_ROPE, :]


def _kv_context_kernel(x_ref, mod_ref, nw_ref, kvn_ref, wdn_ref, k_ref, v_ref):
    shift, scale = mod_ref[0, 0, 0:1, :], mod_ref[0, 0, 1:2, :]
    h = _rms_mod(x_ref[0], nw_ref[...], shift, scale).astype(BF16)
    down = _dot(h, wdn_ref[...])
    ckv = _rms(down[:, Q_LORA:Q_LORA + KV_LORA], kvn_ref[...])
    kr0 = Q_LORA + KV_LORA
    k_rope = down[:, kr0:kr0 + QK_ROPE].astype(BF16)
    _kv_store(ckv, k_rope, k_ref, v_ref)


def _mla_qkv(x, ctx, mods, norm_w, q_norm, kv_norm, tc, ts, tct, tst, w_down, w_uq_t, tq, tk):
    b, n, d = x.shape
    n_ctx = ctx.shape[1]
    tm = tq
    assert n % tm == 0 and tm % tk == 0
    q_t, k, v_t = pl.pallas_call(
        _qkv_latent_kernel,
        grid=(n // tm, b),
        in_specs=[
            pl.BlockSpec((1, tm, d), lambda j, bi: (bi, j, 0)),
            _mod_spec(0, lambda j, bi: bi),
            _resident((1, d)),
            _resident((1, Q_LORA)),
            _resident((1, KV_LORA)),
            pl.BlockSpec((tm, QK_ROPE), lambda j, bi: (j, 0)),
            pl.BlockSpec((tm, QK_ROPE), lambda j, bi: (j, 0)),
            pl.BlockSpec((QK_ROPE, tm), lambda j, bi: (0, j)),
            pl.BlockSpec((QK_ROPE, tm), lambda j, bi: (0, j)),
            _resident(w_down.shape),
            _resident(w_uq_t.shape),
        ],
        out_specs=[
            pl.BlockSpec((1, N_HEADS, 1, QK_DIM, tm), lambda j, bi: (bi, 0, j, 0, 0)),
            pl.BlockSpec((1, tm, QK_DIM), lambda j, bi: (bi, j, 0)),
            pl.BlockSpec((1, tm // tk, V_ROWS, tk), lambda j, bi: (bi, j, 0, 0)),
        ],
        out_shape=[jax.ShapeDtypeStruct((b, N_HEADS, n // tq, QK_DIM, tq), BF16),
                   jax.ShapeDtypeStruct((b, n, QK_DIM), BF16),
                   jax.ShapeDtypeStruct((b, n // tk, V_ROWS, tk), BF16)],
        compiler_params=_params("arbitrary", "arbitrary"),
        name="mla_qkv_latent",
    )(x, mods, norm_w, q_norm, kv_norm, tc, ts, tct, tst, w_down, w_uq_t)
    k_ctx, v_ctx_t = pl.pallas_call(
        _kv_context_kernel,
        grid=(b,),
        in_specs=[
            pl.BlockSpec((1, n_ctx, d), lambda bi: (bi, 0, 0)),
            _mod_spec(0, lambda bi: b),
            _resident((1, d)),
            _resident((1, KV_LORA)),
            _resident(w_down.shape),
        ],
        out_specs=[
            pl.BlockSpec((1, n_ctx, QK_DIM), lambda bi: (bi, 0, 0)),
            pl.BlockSpec((1, 1, V_ROWS, n_ctx), lambda bi: (bi, 0, 0, 0)),
        ],
        out_shape=[jax.ShapeDtypeStruct((b, n_ctx, QK_DIM), BF16),
                   jax.ShapeDtypeStruct((b, 1, V_ROWS, n_ctx), BF16)],
        compiler_params=_params("arbitrary"),
        name="mla_kv_context",
    )(ctx, mods, norm_w, kv_norm, w_down)
    return q_t, k, v_t, k_ctx, v_ctx_t


def _attn_kernel(q_ref, k_ref, v_ref, kc_ref, vc_ref, o_ref, *, tk):
    n_tiles, tq = q_ref.shape[2], q_ref.shape[4]
    n_chunks = v_ref.shape[1]

    def scores(t, c):
        keys = kc_ref[0] if c < 0 else k_ref[0, c * tk:(c + 1) * tk, :]
        return _dot(keys, q_ref[0, 0, t])

    def values(c):
        return vc_ref[0, 0] if c < 0 else v_ref[0, c]

    def finish(t, acc):
        o_ref[0, :, t * tq:(t + 1) * tq] = (acc[0:V_HEAD, :] / acc[V_HEAD:V_HEAD + 1, :]).astype(BF16)

    items = [(t, c) for t in range(n_tiles) for c in range(-1, n_chunks)]
    ahead = [scores(*it) for it in items[:2]]
    over_by_tile = []
    for idx, (t, c) in enumerate(items):
        s = ahead.pop(0)
        if idx + 2 < len(items):
            ahead.append(scores(*items[idx + 2]))
        chunk_max = jnp.max(s, axis=0, keepdims=True)
        if c < 0:
            ref = chunk_max
            acc = _dot(values(c), jnp.exp2(s - ref).astype(BF16))
            over = jnp.zeros((1, tq), F32)
        else:
            p = jnp.exp2(s - ref).astype(BF16)
            acc = jnp.exp2(acc_ref - ref) * acc + _dot(values(c), p)
            over = jnp.maximum(over, chunk_max - ref)
        acc_ref = ref
        ref = jnp.maximum(ref, chunk_max)
        if c == n_chunks - 1:
            finish(t, acc)
            over_by_tile.append(over)

    for t, over in enumerate(over_by_tile):
        @pl.when(jnp.max(over) > LAG_LIMIT)
        def _(t=t):
            q_t = q_ref[0, 0, t]

            def body(c, carry):
                m, acc = carry
                k0 = pl.multiple_of(c * tk, tk)
                s = _dot(k_ref[0, pl.ds(k0, tk), :], q_t)
                m_new = jnp.maximum(m, jnp.max(s, axis=0, keepdims=True))
                p = jnp.exp2(s - m_new).astype(BF16)
                acc = jnp.exp2(m - m_new) * acc + _dot(v_ref[0, c], p)
                return m_new, acc

            s0 = _dot(kc_ref[0], q_t)
            m0 = jnp.max(s0, axis=0, keepdims=True)
            acc0 = _dot(vc_ref[0, 0], jnp.exp2(s0 - m0).astype(BF16))
            finish(t, lax.fori_loop(0, n_chunks, body, (m0, acc0))[1])


def _attention(q_t, k, v_t, k_ctx, v_ctx_t):
    b, nh, n_tiles, _, tq = q_t.shape
    n = n_tiles * tq
    per_step = math.gcd(ATTN_TILES_PER_STEP, n_tiles)
    n_chunks, tk = v_t.shape[1], v_t.shape[3]
    n_ctx = k_ctx.shape[1]
    batch3 = lambda bi, hi, i: (bi, 0, 0)
    batch4 = lambda bi, hi, i: (bi, 0, 0, 0)
    return pl.pallas_call(
        functools.partial(_attn_kernel, tk=tk),
        grid=(b, nh, n_tiles // per_step),
        in_specs=[
            pl.BlockSpec((1, 1, per_step, QK_DIM, tq), lambda bi, hi, i: (bi, hi, i, 0, 0)),
            pl.BlockSpec((1, n, QK_DIM), batch3),
            pl.BlockSpec((1, n_chunks, V_ROWS, tk), batch4),
            pl.BlockSpec((1, n_ctx, QK_DIM), batch3),
            pl.BlockSpec((1, 1, V_ROWS, n_ctx), batch4),
        ],
        out_specs=pl.BlockSpec((1, V_HEAD, per_step * tq), lambda bi, hi, i: (bi, hi, i)),
        out_shape=jax.ShapeDtypeStruct((b, nh * V_HEAD, n), BF16),
        compiler_params=_params("arbitrary", "arbitrary", "arbitrary"),
        name="mla_attention",
    )(q_t, k, v_t, k_ctx, v_ctx_t)


def _swiglu_residual(x, mod_ref, nw_ref, wg_ref, wu_ref, wd_ref):
    shift, scale, gate = mod_ref[0, 0, 3:4, :], mod_ref[0, 0, 4:5, :], mod_ref[0, 0, 5:6, :]
    h = _rms_mod(x, nw_ref[...], shift, scale).astype(BF16)
    y = None
    for c0 in range(0, wg_ref.shape[2], FF_CHUNK):
        g = _dot(h, wg_ref[0, :, c0:c0 + FF_CHUNK])
        u = _dot(h, wu_ref[0, :, c0:c0 + FF_CHUNK])
        act = (g * jax.nn.sigmoid(g) * u).astype(BF16)
        part = _dot(act, wd_ref[0, c0:c0 + FF_CHUNK, :])
        y = part if y is None else y + part
    return x + gate * y


def _proj_ffn_kernel(x_ref, o_ref, mod_ref, nw_ref, wo_ref, wg_ref, wu_ref, wd_ref, out_ref):
    x = x_ref[0] + mod_ref[0, 0, 2:3, :] * _dot_tn(o_ref[0], wo_ref[0])
    out_ref[0] = _swiglu_residual(x, mod_ref, nw_ref, wg_ref, wu_ref, wd_ref)


def _proj_ffn(x, o_t, mods, layer, norm_w, w_o, w_gate, w_up, w_down, tm):
    b, n, d = x.shape
    tile = pl.BlockSpec((1, tm, d), lambda bi, i: (bi, i, 0))
    return pl.pallas_call(
        _proj_ffn_kernel,
        grid=(b, n // tm),
        in_specs=[tile, pl.BlockSpec((1, d, tm), lambda bi, i: (bi, 0, i)), _mod_spec(layer, lambda bi, i: bi),
                  _resident((1, d)), _resident_layer(w_o.shape, 0)]
        + [_resident_layer(w.shape, layer) for w in (w_gate, w_up, w_down)],
        out_specs=tile,
        out_shape=jax.ShapeDtypeStruct((b, n, d), F32),
        compiler_params=_params("arbitrary", "arbitrary"),
        name="attn_proj_ffn",
    )(x, o_t, mods, norm_w, w_o, w_gate, w_up, w_down)


def _conv_ffn_kernel(x_ref, xp_ref, xn_ref, mod_ref, nw_ref, win_ref, cw_ref, wout_ref,
                     nw2_ref, wg_ref, wu_ref, wd_ref, fw_ref, out_ref, z_ref):
    i = pl.program_id(1)
    last = pl.num_programs(1) - 1
    d = x_ref.shape[2]
    tm = x_ref.shape[1]
    x = x_ref[0]
    xe = jnp.concatenate([xp_ref[0], x, xn_ref[0]], axis=0)
    shift, scale, gate = mod_ref[0, 0, 0:1, :], mod_ref[0, 0, 1:2, :], mod_ref[0, 0, 2:3, :]
    h = _rms_mod(xe, nw_ref[...], shift, scale).astype(BF16)
    p = _dot(h, win_ref[0])
    z = p[:, d:2 * d] * p[:, 2 * d:3 * d]
    row = lax.broadcasted_iota(jnp.int32, (tm + 2 * HALO, 1), 0)
    outside = ((row < HALO) & (i == 0)) | ((row >= tm + HALO) & (i == last))
    z_ref[...] = jnp.where(outside, 0.0, z)
    u = (cw_ref[0:1, :] * z_ref[HALO - 1:HALO - 1 + tm, :]
         + cw_ref[1:2, :] * z_ref[HALO:HALO + tm, :]
         + cw_ref[2:3, :] * z_ref[HALO + 1:HALO + 1 + tm, :])
    y = _dot((p[HALO:HALO + tm, 0:d] * u).astype(BF16), wout_ref[0])
    x = _swiglu_residual(x + gate * y, mod_ref, nw2_ref, wg_ref, wu_ref, wd_ref)
    out_ref[0] = _rms(x, fw_ref[...])


def _conv_ffn_final(x, mods, layer, norm_w, w_in, conv_w, w_out, norm_w2, w_gate, w_up, w_down, final_w, tm):
    b, n, d = x.shape
    nblk = n // HALO
    per = tm // HALO
    return pl.pallas_call(
        _conv_ffn_kernel,
        grid=(b, n // tm),
        in_specs=[
            pl.BlockSpec((1, tm, d), lambda bi, i: (bi, i, 0)),
            pl.BlockSpec((1, HALO, d), lambda bi, i: (bi, jnp.maximum(i * per - 1, 0), 0)),
            pl.BlockSpec((1, HALO, d), lambda bi, i: (bi, jnp.minimum((i + 1) * per, nblk - 1), 0)),
            _mod_spec(layer, lambda bi, i: bi),
            _resident((1, d)),
            _resident_layer(w_in.shape, 0),
            _resident(conv_w.shape),
            _resident_layer(w_out.shape, 0),
            _resident((1, d)),
        ] + [_resident_layer(w.shape, layer) for w in (w_gate, w_up, w_down)] + [_resident((1, d))],
        out_specs=pl.BlockSpec((1, tm, d), lambda bi, i: (bi, i, 0)),
        out_shape=jax.ShapeDtypeStruct((b, n, d), F32),
        scratch_shapes=[pltpu.VMEM((tm + 2 * HALO, d), F32)],
        compiler_params=_params("arbitrary", "arbitrary"),
        name="conv_ffn_final",
    )(x, x, x, mods, norm_w, w_in, conv_w, w_out, norm_w2, w_gate, w_up, w_down, final_w)


def _rope_tables(n):
    rows = n // GRID_W
    n_freq = QK_ROPE // 4
    freqs = ROPE_THETA ** (-jnp.arange(n_freq, dtype=F32) / n_freq)
    ar = jnp.arange(rows).astype(F32)[:, None] * freqs
    ac = jnp.arange(GRID_W).astype(F32)[:, None] * freqs

    def table(fn):
        r = jnp.broadcast_to(fn(ar)[:, None, :], (rows, GRID_W, n_freq)).reshape(n, n_freq)
        c = jnp.broadcast_to(fn(ac)[None, :, :], (rows, GRID_W, n_freq)).reshape(n, n_freq)
        return jnp.concatenate([r, r, c, c], axis=-1)

    return table(jnp.cos), table(jnp.sin)


def _rotate_cols(w):
    q = QK_ROPE // 4
    perm = np.concatenate([np.arange(q, 2 * q), np.arange(0, q), np.arange(3 * q, 4 * q), np.arange(2 * q, 3 * q)])
    sign = np.concatenate([-np.ones(q), np.ones(q), -np.ones(q), np.ones(q)]).astype(np.float32)
    return w[..., perm] * sign


def kernel(x, c, ctx, c_ctx, ada_w, ada_b, norm_mix, norm_ffn, mla_w_dq, mla_q_norm, mla_w_uq, mla_w_dkv,
           mla_kv_norm, mla_w_ukv, mla_w_o, conv_w_in, conv_w, conv_w_out, ffn_w_gate, ffn_w_up, ffn_w_down,
           final_norm):
    b, n, d = x.shape

    pad = (-(b + 1)) % 8
    c_rows = jnp.concatenate([c, c_ctx[None, :], jnp.zeros((pad, d), F32)], axis=0)
    mods = _modulation(c_rows, ada_w, ada_b)
    mods = mods.reshape(mods.shape[0], mods.shape[1], 6, d)

    w_uq = mla_w_uq[0].reshape(Q_LORA, N_HEADS, QK_DIM)
    w_ukv = mla_w_ukv[0].reshape(KV_LORA, N_HEADS, QK_NOPE + V_HEAD).transpose(1, 0, 2)
    w_qabs_t, w_o_fold = _fold_weights(w_uq[:, :, :QK_NOPE].transpose(1, 0, 2), w_ukv[:, :, :QK_NOPE],
                                       w_ukv[:, :, QK_NOPE:], mla_w_o[0].reshape(N_HEADS, V_HEAD, d))
    uq_rope = w_uq[:, :, QK_NOPE:]
    w_uq_t = jnp.concatenate([w_qabs_t.reshape(N_HEADS * KV_LORA, Q_LORA), uq_rope.reshape(Q_LORA, -1).T,
                              _rotate_cols(uq_rope).reshape(Q_LORA, -1).T], axis=0).astype(BF16)
    w_dkv = mla_w_dkv[0]
    w_down = jnp.concatenate([mla_w_dq[0], w_dkv, _rotate_cols(w_dkv[:, KV_LORA:])], axis=1).astype(BF16)
    w_o_fold = w_o_fold.reshape(1, N_HEADS * KV_LORA, d).astype(BF16)
    tc, ts = _rope_tables(n)
    tct, tst = tc.T, ts.T

    q_t, k, v_t, k_ctx, v_ctx_t = _mla_qkv(x, ctx, mods, norm_mix[0][None], mla_q_norm[0][None],
                                           mla_kv_norm[0][None], tc, ts, tct, tst, w_down, w_uq_t,
                                           tq=min(ATTN_TQ, n), tk=min(ATTN_TK, n))
    o_t = _attention(q_t, k, v_t, k_ctx, v_ctx_t)
    w_gate, w_up, w_dn = ffn_w_gate.astype(BF16), ffn_w_up.astype(BF16), ffn_w_down.astype(BF16)
    x = _proj_ffn(x, o_t, mods, 0, norm_ffn[0][None], w_o_fold, w_gate, w_up, w_dn, tm=min(1024, n))

    return _conv_ffn_final(x, mods, 1, norm_mix[1][None], conv_w_in.astype(BF16), conv_w[0], conv_w_out.astype(BF16),
                           norm_ffn[1][None], w_gate, w_up, w_dn, final_norm[None], tm=min(512, n))
```
